```python
import math
import jax, jax.numpy as jnp
from jax import lax
import numpy as np

D_MODEL = 1024
BATCH = 2
SEQ = 8192
DEPTH = 2

CHUNK = 64

A_HEADS = 4
A_HEAD_DIM = 64
A_DIM = A_HEADS * A_HEAD_DIM
LEFT_CHUNKS = 8
BAND = (LEFT_CHUNKS + 1) * CHUNK
MAX_REL = 128
N_REL = 2 * MAX_REL + 1

B_HEADS = 8
B_HEAD_DIM = 64
B_DIM = B_HEADS * B_HEAD_DIM
B_GROUPS = 2
B_STATE = 128
B_CONV = 4
XBC_DIM = B_DIM + 2 * B_GROUPS * B_STATE

C_DIM = 256
C_KERNEL = 31

D_MIX = A_DIM + B_DIM + C_DIM

OFF_Q = 0
OFF_K = OFF_Q + A_DIM
OFF_V = OFF_K + A_DIM
OFF_Z = OFF_V + A_DIM
OFF_XBC = OFF_Z + B_DIM
OFF_DT = OFF_XBC + XBC_DIM
OFF_GLU = OFF_DT + B_HEADS
N_IN = OFF_GLU + 2 * C_DIM

D_FF = int(math.ceil(8 * D_MODEL / 3 / 256) * 256)

EPS = 1e-6
NEG_INF = -1e30

kernel_name = "hybrid_chunked_attn_ssd_conformer_conv"


def rms_norm(x, g):
    xf = x.astype(jnp.float32)
    y = xf * lax.rsqrt(jnp.mean(xf * xf, axis=-1, keepdims=True) + EPS)
    return (y * g.astype(jnp.float32)).astype(x.dtype)


def layer_norm(x, g, b):
    xf = x.astype(jnp.float32)
    mu = jnp.mean(xf, axis=-1, keepdims=True)
    var = jnp.mean(jnp.square(xf - mu), axis=-1, keepdims=True)
    y = (xf - mu) * lax.rsqrt(var + EPS)
    return (y * g.astype(jnp.float32) + b.astype(jnp.float32)).astype(x.dtype)


def causal_dwconv(x, w, b):
    k, c = w.shape
    xp = jnp.pad(x, ((0, 0), (k - 1, 0), (0, 0)))
    y = lax.conv_general_dilated(
        xp, w.astype(x.dtype)[:, None, :], window_strides=(1,), padding="VALID",
        dimension_numbers=("NWC", "WIO", "NWC"), feature_group_count=c)
    return y + b.astype(x.dtype)


def chunked_rel_attention(q, k, v, rel_bias):
    bsz, s, h, dh = q.shape
    nc = s // CHUNK
    qc = q.reshape(bsz, nc, CHUNK, h, dh)
    pad = ((0, 0), (LEFT_CHUNKS * CHUNK, 0), (0, 0), (0, 0))
    kp = jnp.pad(k, pad).reshape(bsz, nc + LEFT_CHUNKS, CHUNK, h, dh)
    vp = jnp.pad(v, pad).reshape(bsz, nc + LEFT_CHUNKS, CHUNK, h, dh)
    band_idx = jnp.arange(nc)[:, None] + jnp.arange(LEFT_CHUNKS + 1)[None, :]
    kb = kp[:, band_idx].reshape(bsz, nc, BAND, h, dh)
    vb = vp[:, band_idx].reshape(bsz, nc, BAND, h, dh)
    scores = jnp.einsum("bcqhd,bckhd->bhcqk", qc, kb).astype(jnp.float32) * (dh ** -0.5)
    qi = jnp.arange(CHUNK)
    kj = jnp.arange(BAND)
    rel = qi[:, None] + LEFT_CHUNKS * CHUNK - kj[None, :]
    rel = jnp.clip(rel, -MAX_REL, MAX_REL) + MAX_REL
    bias = rel_bias.astype(jnp.float32)[:, rel]
    valid = (jnp.arange(nc)[:, None] + kj[None, :] // CHUNK) >= LEFT_CHUNKS
    scores = scores + bias[None, :, None]
    scores = jnp.where(valid[None, None, :, None, :], scores, NEG_INF)
    p = jax.nn.softmax(scores, axis=-1).astype(v.dtype)
    o = jnp.einsum("bhcqk,bckhd->bcqhd", p, vb)
    return o.reshape(bsz, s, h * dh)


def ssd_scan(x, a, bm, cm):
    bsz, s, h, p = x.shape
    nc = s // CHUNK
    rep = h // bm.shape[2]
    bh = jnp.repeat(bm, rep, axis=2).astype(jnp.float32)
    ch = jnp.repeat(cm, rep, axis=2).astype(jnp.float32)
    n = bh.shape[-1]
    xc = x.astype(jnp.float32).reshape(bsz, nc, CHUNK, h, p)
    bc = bh.reshape(bsz, nc, CHUNK, h, n)
    cc = ch.reshape(bsz, nc, CHUNK, h, n)
    ac = a.astype(jnp.float32).reshape(bsz, nc, CHUNK, h)
    a_cs = jnp.cumsum(ac, axis=2)
    causal = jnp.tril(jnp.ones((CHUNK, CHUNK), dtype=bool))[None, None, :, :, None]
    seg = a_cs[:, :, :, None, :] - a_cs[:, :, None, :, :]
    decay = jnp.exp(jnp.where(causal, seg, -jnp.inf))
    scores = jnp.einsum("bclhn,bcshn->bclsh", cc, bc) * decay
    y_diag = jnp.einsum("bclsh,bcshp->bclhp", scores, xc)
    decay_to_end = jnp.exp(a_cs[:, :, -1:, :] - a_cs)
    chunk_states = jnp.einsum("bclhn,bclh,bclhp->bchpn", bc, decay_to_end, xc)
    chunk_decay = jnp.exp(a_cs[:, :, -1, :])

    def step(state, inp):
        s_c, d_c = inp
        return state * d_c[..., None, None] + s_c, state

    init = jnp.zeros((bsz, h, p, n), dtype=chunk_states.dtype)
    _, prev = lax.scan(step, init, (jnp.moveaxis(chunk_states, 1, 0), jnp.moveaxis(chunk_decay, 1, 0)))
    prev = jnp.moveaxis(prev, 0, 1)
    y_off = jnp.einsum("bclhn,bchpn,bclh->bclhp", cc, prev, jnp.exp(a_cs))
    return (y_diag + y_off).reshape(bsz, s, h, p)


def ssd_mixer(z, xbc, dt_raw, conv_w, conv_b, dt_bias, a_log, d_skip, norm_g):
    bsz, s, _ = z.shape
    xbc = jax.nn.silu(causal_dwconv(xbc, conv_w, conv_b))
    xs = xbc[..., :B_DIM].reshape(bsz, s, B_HEADS, B_HEAD_DIM)
    bm = xbc[..., B_DIM:B_DIM + B_GROUPS * B_STATE].reshape(bsz, s, B_GROUPS, B_STATE)
    cm = xbc[..., B_DIM + B_GROUPS * B_STATE:].reshape(bsz, s, B_GROUPS, B_STATE)
    dt = jax.nn.softplus(dt_raw.astype(jnp.float32) + dt_bias.astype(jnp.float32))
    a = -jnp.exp(a_log.astype(jnp.float32))
    y = ssd_scan(xs.astype(jnp.float32) * dt[..., None], dt * a, bm, cm)
    y = y + xs.astype(jnp.float32) * d_skip.astype(jnp.float32)[:, None]
    y = y.reshape(bsz, s, B_DIM) * jax.nn.silu(z.astype(jnp.float32))
    yg = y.reshape(bsz, s, B_GROUPS, B_DIM // B_GROUPS)
    yg = yg * lax.rsqrt(jnp.mean(yg * yg, axis=-1, keepdims=True) + EPS)
    y = yg.reshape(bsz, s, B_DIM) * norm_g.astype(jnp.float32)
    return y.astype(z.dtype)


def conformer_conv(u, dw_w, dw_b, ln_g, ln_b):
    a, g = jnp.split(u, 2, axis=-1)
    h = a * jax.nn.sigmoid(g)
    h = causal_dwconv(h, dw_w, dw_b)
    h = layer_norm(h, ln_g, ln_b)
    return jax.nn.silu(h)


def setup_inputs(seed: int = 0) -> dict:
    key = jax.random.key(seed)
    ks = jax.random.split(key, 24)
    f32 = jnp.float32
    L = DEPTH
    nrm = lambda k, shape, scale: jax.random.normal(k, shape, f32) * scale
    dt0 = jnp.exp(jax.random.uniform(ks[8], (L, B_HEADS), f32, math.log(1e-3), math.log(1e-1)))
    return {
        "x": nrm(ks[0], (BATCH, SEQ, D_MODEL), 1.0),
        "norm1_g": 1.0 + nrm(ks[1], (L, D_MODEL), 0.02),
        "w_in": nrm(ks[2], (L, D_MODEL, N_IN), D_MODEL ** -0.5),
        "attn_rel_bias": nrm(ks[3], (L, A_HEADS, N_REL), 0.5),
        "ssm_conv_w": nrm(ks[4], (L, B_CONV, XBC_DIM), B_CONV ** -0.5),
        "ssm_conv_b": nrm(ks[5], (L, XBC_DIM), 0.02),
        "ssm_dt_bias": dt0 + jnp.log(-jnp.expm1(-dt0)),
        "ssm_a_log": jnp.log(jax.random.uniform(ks[6], (L, B_HEADS), f32, 1.0, 16.0)),
        "ssm_d": 1.0 + nrm(ks[7], (L, B_HEADS), 0.1),
        "ssm_norm_g": 1.0 + nrm(ks[9], (L, B_DIM), 0.02),
        "conv_dw_w": nrm(ks[10], (L, C_KERNEL, C_DIM), C_KERNEL ** -0.5),
        "conv_dw_b": nrm(ks[11], (L, C_DIM), 0.02),
        "conv_ln_g": 1.0 + nrm(ks[12], (L, C_DIM), 0.02),
        "conv_ln_b": nrm(ks[13], (L, C_DIM), 0.02),
        "w_out": nrm(ks[14], (L, D_MIX, D_MODEL), D_MIX ** -0.5),
        "norm2_g": 1.0 + nrm(ks[15], (L, D_MODEL), 0.02),
        "ffn_w_gate": nrm(ks[16], (L, D_MODEL, D_FF), D_MODEL ** -0.5),
        "ffn_w_up": nrm(ks[17], (L, D_MODEL, D_FF), D_MODEL ** -0.5),
        "ffn_w_down": nrm(ks[18], (L, D_FF, D_MODEL), D_FF ** -0.5),
        "final_norm_g": 1.0 + nrm(ks[19], (D_MODEL,), 0.02),
    }


def reference(x, norm1_g, w_in, attn_rel_bias, ssm_conv_w, ssm_conv_b, ssm_dt_bias,
              ssm_a_log, ssm_d, ssm_norm_g, conv_dw_w, conv_dw_b, conv_ln_g, conv_ln_b,
              w_out, norm2_g, ffn_w_gate, ffn_w_up, ffn_w_down, final_norm_g):
    bsz, s, _ = x.shape
    for l in range(DEPTH):
        h = rms_norm(x, norm1_g[l])
        proj = h @ w_in[l]
        q = proj[..., OFF_Q:OFF_K].reshape(bsz, s, A_HEADS, A_HEAD_DIM)
        k = proj[..., OFF_K:OFF_V].reshape(bsz, s, A_HEADS, A_HEAD_DIM)
        v = proj[..., OFF_V:OFF_Z].reshape(bsz, s, A_HEADS, A_HEAD_DIM)
        out_a = chunked_rel_attention(q, k, v, attn_rel_bias[l])
        out_b = ssd_mixer(proj[..., OFF_Z:OFF_XBC], proj[..., OFF_XBC:OFF_DT],
                          proj[..., OFF_DT:OFF_GLU], ssm_conv_w[l], ssm_conv_b[l],
                          ssm_dt_bias[l], ssm_a_log[l], ssm_d[l], ssm_norm_g[l])
        out_c = conformer_conv(proj[..., OFF_GLU:N_IN], conv_dw_w[l], conv_dw_b[l],
                               conv_ln_g[l], conv_ln_b[l])
        mix = jnp.concatenate([out_a, out_b, out_c], axis=-1)
        x = x + mix @ w_out[l]
        h = rms_norm(x, norm2_g[l])
        x = x + (jax.nn.silu(h @ ffn_w_gate[l]) * (h @ ffn_w_up[l])) @ ffn_w_down[l]
    return rms_norm(x, final_norm_g)
```

```python
import functools

import jax
import jax.numpy as jnp
from jax import lax
from jax.experimental import pallas as pl
from jax.experimental.pallas import tpu as pltpu

F32 = jnp.float32
BF16 = jnp.bfloat16

D_MODEL = 1024
CHUNK = 64
A_HEADS = 4
A_DIM = 256
LEFT_CHUNKS = 8
LEFT = LEFT_CHUNKS * CHUNK
MAX_REL = 128
N_REL = 2 * MAX_REL + 1
B_HEADS = 8
B_HEAD_DIM = 64
B_DIM = 512
B_GROUPS = 2
B_STATE = 128
B_CONV = 4
XBC_DIM = 1024
C_DIM = 256
C_KERNEL = 31
D_MIX = 1024
D_FF = 2816
EPS = 1e-6
NEG_INF = -1e30

OFF_QKV = 0
OFF_Z = 768
OFF_XBC = 1280
OFF_GLU = 2304
OFF_DT = 2816
N_PROJ = 2944
DT_PAD = 128

TS = 512
QB = 128
BAND = LEFT + QB
XBC_HALO = 8
GLU_HALO = 32
TM = 512
FF_SLICES = ((0, 1024), (1024, 2048), (2048, 2816))
VMEM_LIMIT = 56 * 1024 * 1024


def _dot(a, b):
    return jnp.dot(a, b, preferred_element_type=F32)


def _dot_t(a, b):
    return lax.dot_general(a, b, (((1,), (1,)), ((), ())), preferred_element_type=F32)


def _tdot(a, b):
    return lax.dot_general(a, b, (((0,), (0,)), ((), ())), preferred_element_type=F32)


def _split3(a):
    hi = a.astype(BF16)
    r1 = a - hi.astype(F32)
    mid = r1.astype(BF16)
    lo = (r1 - mid.astype(F32)).astype(BF16)
    return hi, mid, lo


def _sel_right(a, m01):
    hi, mid, lo = _split3(a)
    return (_dot(lo, m01) + _dot(mid, m01)) + _dot(hi, m01)


def _sel_left(m01, a):
    hi, mid, lo = _split3(a)
    return (_dot(m01, lo) + _dot(m01, mid)) + _dot(m01, hi)


def _rms(x, g):
    return x * lax.rsqrt(jnp.mean(x * x, axis=-1, keepdims=True) + EPS) * g


def _silu(x):
    return x * jax.nn.sigmoid(x)


def _iota(shape, dim):
    return lax.broadcasted_iota(jnp.int32, shape, dim)


def _mixer_kernel(x_ref, n1g_ref, w_ref, relb_ref, convw_ref, convb_ref, dtb_ref, alog_ref,
                  dskip_ref, normg_ref, dww_ref, dwb_ref, lng_ref, lnb_ref,
                  o_ref,
                  q_s, k_s, v_s, bias_s, z_s, xbc_s, xc_s, glu_s, dte_s, st_s):
    b = pl.program_id(0)
    i = pl.program_id(1)

    @pl.when(jnp.logical_and(b == 0, i == 0))
    def _build_bias():
        rr = _iota((2 * QB, BAND), 0)
        c = _iota((2 * QB, BAND), 1)
        r = jnp.bitwise_and(rr, QB - 1)
        second = rr >= QB
        idx = jnp.clip(LEFT + r - c, -MAX_REL, MAX_REL) + MAX_REL
        qc = jnp.right_shift(r, 6)
        kc = jnp.right_shift(c, 6)
        valid = jnp.logical_and(kc >= qc, kc <= qc + LEFT_CHUNKS)
        for pr in range(2):
            def body(v, acc, pr=pr):
                val = jnp.where(second, relb_ref[2 * pr + 1, v], relb_ref[2 * pr, v])
                return jnp.where(idx == v, val, acc)
            acc = lax.fori_loop(0, N_REL, body, jnp.zeros((2 * QB, BAND), F32))
            bias_s[pr] = jnp.where(valid, acc, NEG_INF)

    @pl.when(i == 0)
    def _reset():
        k_s[:, 0:TS, :] = jnp.zeros((2, TS, 128), BF16)
        v_s[:, 0:TS, :] = jnp.zeros((2, TS, 128), BF16)
        xbc_s[0:XBC_HALO, :] = jnp.zeros((XBC_HALO, XBC_DIM), F32)
        glu_s[0:GLU_HALO, :] = jnp.zeros((GLU_HALO, C_DIM), F32)
        st_s[...] = jnp.zeros((B_GROUPS, B_STATE, 256), F32)

    h = _rms(x_ref[0], n1g_ref[...]).astype(BF16)
    qkv = _dot(h, w_ref[:, OFF_QKV:OFF_Z])
    for pr in range(2):
        q_s[pr] = (qkv[:, pr * 128:(pr + 1) * 128] * 0.125).astype(BF16)
        k_s[pr, TS:2 * TS, :] = qkv[:, 256 + pr * 128:256 + (pr + 1) * 128].astype(BF16)
        v_s[pr, TS:2 * TS, :] = qkv[:, 512 + pr * 128:512 + (pr + 1) * 128].astype(BF16)
    z_s[...] = _dot(h, w_ref[:, OFF_Z:OFF_XBC])
    xbc_s[XBC_HALO:XBC_HALO + TS, :] = _dot(h, w_ref[:, OFF_XBC:OFF_GLU])
    u = _dot(h, w_ref[:, OFF_GLU:OFF_DT])
    glu_s[GLU_HALO:GLU_HALO + TS, :] = u[:, :C_DIM] * jax.nn.sigmoid(u[:, C_DIM:])
    dt_raw = _dot(h, w_ref[:, OFF_DT:N_PROJ]) + dtb_ref[...]
    dt = jnp.maximum(dt_raw, 0.0) + jnp.log1p(jnp.exp(-jnp.abs(dt_raw)))
    expand = (jnp.right_shift(_iota((DT_PAD, B_DIM), 1), 6) == _iota((DT_PAD, B_DIM), 0)).astype(BF16)
    dte_s[...] = _sel_right(dt, expand)

    lane = _iota((QB, 128), 1)
    first_head = lane < 64

    def attn_block(j, carry):
        r0 = pl.multiple_of(j * QB, QB)
        col_ok = (_iota((1, BAND), 1) + (i * TS + r0 - LEFT)) >= 0
        for pr in range(2):
            qa = q_s[pr, pl.ds(r0, QB), :]
            qq = jnp.concatenate([jnp.where(first_head, qa, jnp.zeros_like(qa)),
                                  jnp.where(first_head, jnp.zeros_like(qa), qa)], axis=0)
            kb = k_s[pr, pl.ds(r0, BAND), :]
            vb = v_s[pr, pl.ds(r0, BAND), :]
            s = _dot_t(qq, kb) + bias_s[pr]
            s = jnp.where(col_ok, s, NEG_INF)
            m = jnp.max(s, axis=-1, keepdims=True)
            p = jnp.exp(s - m)
            l = jnp.sum(p, axis=-1, keepdims=True)
            o = _dot(p.astype(BF16), vb) / l
            o_ref[0, pl.ds(r0, QB), pr * 128:(pr + 1) * 128] = jnp.where(
                first_head, o[:QB], o[QB:]).astype(BF16)
        return carry

    lax.fori_loop(0, TS // QB, attn_block, 0)

    for c in range(TS // CHUNK):
        acc = jnp.broadcast_to(convb_ref[...], (CHUNK, XBC_DIM))
        for k in range(B_CONV):
            start = XBC_HALO - (B_CONV - 1) + k + c * CHUNK
            acc = acc + xbc_s[start:start + CHUNK, :] * convw_ref[k:k + 1, :]
        xc_s[c * CHUNK:(c + 1) * CHUNK, :] = _silu(acc)

    a_neg = -jnp.exp(alog_ref[...])
    l_i = _iota((CHUNK, B_DIM), 0)
    s_i = jnp.bitwise_and(_iota((CHUNK, B_DIM), 1), CHUNK - 1)
    causal = l_i >= s_i
    diag = (l_i == s_i).astype(F32)
    tri = (_iota((CHUNK, CHUNK), 0) >= _iota((CHUNK, CHUNK), 1)).astype(BF16)
    ones = jnp.ones((CHUNK, CHUNK), BF16)
    blockdiag = (jnp.right_shift(_iota((256, 256), 0), 6) ==
                 jnp.right_shift(_iota((256, 256), 1), 6)).astype(F32)

    def ssd_chunk(c, carry):
        r0 = pl.multiple_of(c * CHUNK, CHUNK)
        xc = xc_s[pl.ds(r0, CHUNK), :]
        xs = xc[:, :B_DIM]
        dte = dte_s[pl.ds(r0, CHUNK), :]
        acs = _sel_left(tri, dte * a_neg)
        acs_row = _sel_left(ones, acs * diag)
        decay = jnp.exp(jnp.where(causal, acs - acs_row, -jnp.inf))
        last = acs[CHUNK - 1:CHUNK, :]
        to_end = jnp.exp(last - acs)
        chunk_decay = jnp.exp(last)
        from_start = jnp.exp(acs)
        xdt = xs * dte
        z = z_s[pl.ds(r0, CHUNK), :]
        for g in range(B_GROUPS):
            cols = slice(g * 256, (g + 1) * 256)
            bg = xc[:, B_DIM + g * B_STATE:B_DIM + (g + 1) * B_STATE].astype(BF16)
            cg = xc[:, B_DIM + B_GROUPS * B_STATE + g * B_STATE:
                    B_DIM + B_GROUPS * B_STATE + (g + 1) * B_STATE].astype(BF16)
            cb = _dot_t(cg, jnp.concatenate([bg] * 4, axis=0))
            scores = (cb * decay[:, cols]).astype(BF16)
            xg = xdt[:, cols]
            x_bd = (jnp.concatenate([xg] * 4, axis=0) * blockdiag).astype(BF16)
            y = _dot(scores, x_bd)
            state = st_s[g]
            y = y + _dot(cg, state.astype(BF16)) * from_start[:, cols]
            st_s[g] = state * chunk_decay[:, cols] + _tdot(bg, (xg * to_end[:, cols]).astype(BF16))
            y = y + xs[:, cols] * dskip_ref[:, cols]
            y = y * _silu(z[:, cols])
            y = y * lax.rsqrt(jnp.mean(y * y, axis=-1, keepdims=True) + EPS) * normg_ref[:, cols]
            o_ref[0, pl.ds(r0, CHUNK), A_DIM + g * 256:A_DIM + (g + 1) * 256] = y.astype(BF16)
        return carry

    lax.fori_loop(0, TS // CHUNK, ssd_chunk, 0)

    for c in range(TS // CHUNK):
        acc = jnp.broadcast_to(dwb_ref[...], (CHUNK, C_DIM))
        for k in range(C_KERNEL):
            start = GLU_HALO - (C_KERNEL - 1) + k + c * CHUNK
            acc = acc + glu_s[start:start + CHUNK, :] * dww_ref[k:k + 1, :]
        mu = jnp.mean(acc, axis=-1, keepdims=True)
        d = acc - mu
        var = jnp.mean(d * d, axis=-1, keepdims=True)
        y = d * lax.rsqrt(var + EPS) * lng_ref[...] + lnb_ref[...]
        o_ref[0, c * CHUNK:(c + 1) * CHUNK, A_DIM + B_DIM:D_MIX] = _silu(y).astype(BF16)

    k_s[:, 0:TS, :] = k_s[:, TS:2 * TS, :]
    v_s[:, 0:TS, :] = v_s[:, TS:2 * TS, :]
    xbc_s[0:XBC_HALO, :] = xbc_s[TS:TS + XBC_HALO, :]
    glu_s[0:GLU_HALO, :] = glu_s[TS:TS + GLU_HALO, :]


def _const_spec(shape):
    nd = len(shape)
    return pl.BlockSpec(shape, lambda b, i: (0,) * nd)


def _mixer_call(x, n1g, w_r, relb, convw, convb, dtb, alog, dskip, normg, dww, dwb, lng, lnb):
    bsz, s, _ = x.shape
    in_specs = [
        pl.BlockSpec((1, TS, D_MODEL), lambda b, i: (b, i, 0)),
        _const_spec((1, D_MODEL)),
        _const_spec((D_MODEL, N_PROJ)),
        pl.BlockSpec(memory_space=pltpu.SMEM),
        _const_spec((B_CONV, XBC_DIM)),
        _const_spec((1, XBC_DIM)),
        _const_spec((1, DT_PAD)),
        _const_spec((1, B_DIM)),
        _const_spec((1, B_DIM)),
        _const_spec((1, B_DIM)),
        _const_spec((C_KERNEL, C_DIM)),
        _const_spec((1, C_DIM)),
        _const_spec((1, C_DIM)),
        _const_spec((1, C_DIM)),
    ]
    scratch = [
        pltpu.VMEM((2, TS, 128), BF16),
        pltpu.VMEM((2, 2 * TS, 128), BF16),
        pltpu.VMEM((2, 2 * TS, 128), BF16),
        pltpu.VMEM((2, 2 * QB, BAND), F32),
        pltpu.VMEM((TS, B_DIM), F32),
        pltpu.VMEM((XBC_HALO + TS, XBC_DIM), F32),
        pltpu.VMEM((TS, XBC_DIM), F32),
        pltpu.VMEM((GLU_HALO + TS, C_DIM), F32),
        pltpu.VMEM((TS, B_DIM), F32),
        pltpu.VMEM((B_GROUPS, B_STATE, 256), F32),
    ]
    return pl.pallas_call(
        _mixer_kernel,
        grid=(bsz, s // TS),
        in_specs=in_specs,
        out_specs=pl.BlockSpec((1, TS, D_MIX), lambda b, i: (b, i, 0)),
        out_shape=jax.ShapeDtypeStruct((bsz, s, D_MIX), BF16),
        scratch_shapes=scratch,
        compiler_params=pltpu.CompilerParams(
            dimension_semantics=("arbitrary", "arbitrary"),
            vmem_limit_bytes=VMEM_LIMIT),
        name="mixer",
    )(x, n1g, w_r, relb, convw, convb, dtb, alog, dskip, normg, dww, dwb, lng, lnb)


def _ffn_kernel(x_ref, mix_ref, wout_ref, n2g_ref, wg_ref, wu_ref, wd_ref, fg_ref, o_ref, *, final):
    xn = x_ref[...] + _dot(mix_ref[...], wout_ref[...])
    h = _rms(xn, n2g_ref[...]).astype(BF16)
    acc = xn
    for lo, hi in FF_SLICES:
        g = _dot(h, wg_ref[:, lo:hi])
        u = _dot(h, wu_ref[:, lo:hi])
        acc = acc + _dot((_silu(g) * u).astype(BF16), wd_ref[lo:hi, :])
    if final:
        acc = _rms(acc, fg_ref[...])
    o_ref[...] = acc


def _ffn_call(x2, mix2, wout, n2g, wg, wu, wd, fg, final):
    t = x2.shape[0]

    def resident(shape):
        return pl.BlockSpec(shape, lambda i: (0, 0), pipeline_mode=pl.Buffered(1))

    return pl.pallas_call(
        functools.partial(_ffn_kernel, final=final),
        grid=(t // TM,),
        in_specs=[
            pl.BlockSpec((TM, D_MODEL), lambda i: (i, 0)),
            pl.BlockSpec((TM, D_MIX), lambda i: (i, 0)),
            resident((D_MIX, D_MODEL)),
            resident((1, D_MODEL)),
            resident((D_MODEL, D_FF)),
            resident((D_MODEL, D_FF)),
            resident((D_FF, D_MODEL)),
            resident((1, D_MODEL)),
        ],
        out_specs=pl.BlockSpec((TM, D_MODEL), lambda i: (i, 0)),
        out_shape=jax.ShapeDtypeStruct((t, D_MODEL), F32),
        compiler_params=pltpu.CompilerParams(
            dimension_semantics=("arbitrary",),
            vmem_limit_bytes=VMEM_LIMIT),
        name="ffn_final" if final else "ffn",
    )(x2, mix2, wout, n2g, wg, wu, wd, fg)


def kernel(x, norm1_g, w_in, attn_rel_bias, ssm_conv_w, ssm_conv_b, ssm_dt_bias, ssm_a_log, ssm_d, ssm_norm_g, conv_dw_w, conv_dw_b, conv_ln_g, conv_ln_b, w_out, norm2_g, ffn_w_gate, ffn_w_up, ffn_w_down, final_norm_g):
    bsz, s, d = x.shape
    depth = w_in.shape[0]
    assert d == D_MODEL and s % TS == 0 and (bsz * s) % TM == 0
    row = lambda v: v.reshape(1, -1).astype(F32)
    per_head = lambda v: jnp.repeat(v.astype(F32), B_HEAD_DIM).reshape(1, B_DIM)
    for l in range(depth):
        w_r = jnp.concatenate(
            [w_in[l, :, :OFF_GLU], w_in[l, :, OFF_GLU + B_HEADS:],
             jnp.pad(w_in[l, :, OFF_GLU:OFF_GLU + B_HEADS], ((0, 0), (0, DT_PAD - B_HEADS)))],
            axis=1).astype(BF16)
        dtb = jnp.pad(ssm_dt_bias[l].astype(F32), (0, DT_PAD - B_HEADS)).reshape(1, DT_PAD)
        mix = _mixer_call(
            x, row(norm1_g[l]), w_r, attn_rel_bias[l].astype(F32),
            ssm_conv_w[l].astype(F32), row(ssm_conv_b[l]), dtb,
            per_head(ssm_a_log[l]), per_head(ssm_d[l]), row(ssm_norm_g[l]),
            conv_dw_w[l].astype(F32), row(conv_dw_b[l]), row(conv_ln_g[l]), row(conv_ln_b[l]))
        x = _ffn_call(
            x.reshape(bsz * s, d), mix.reshape(bsz * s, D_MIX),
            w_out[l].astype(BF16), row(norm2_g[l]),
            ffn_w_gate[l].astype(BF16), ffn_w_up[l].astype(BF16), ffn_w_down[l].astype(BF16),
            row(final_norm_g), final=(l == depth - 1)).reshape(bsz, s, d)
    return x
```

```python
import functools

import jax
import jax.numpy as jnp
from jax import lax
from jax.experimental import pallas as pl
from jax.experimental.pallas import tpu as pltpu

F32 = jnp.float32
BF16 = jnp.bfloat16

D_MODEL = 1024
CHUNK = 64
A_HEADS = 4
A_DIM = 256
LEFT_CHUNKS = 8
LEFT = LEFT_CHUNKS * CHUNK
MAX_REL = 128
N_REL = 2 * MAX_REL + 1
B_HEADS = 8
B_HEAD_DIM = 64
B_DIM = 512
B_GROUPS = 2
B_STATE = 128
B_CONV = 4
XBC_DIM = 1024
C_DIM = 256
C_KERNEL = 31
D_MIX = 1024
D_FF = 2816
EPS = 1e-6
NEG_INF = -1e30

OFF_QKV = 0
OFF_Z = 768
OFF_XBC = 1280
OFF_GLU = 2304
OFF_DT = 2816
N_PROJ = 2944
DT_PAD = 128

TS = 512
QB = 128
BAND = LEFT + QB
XBC_HALO = 8
GLU_HALO = 32
TM = 512
FF_SLICES = ((0, 1024), (1024, 2048), (2048, 2816))
VMEM_LIMIT = 56 * 1024 * 1024


def _dot(a, b):
    return jnp.dot(a, b, preferred_element_type=F32)


def _dot_t(a, b):
    return lax.dot_general(a, b, (((1,), (1,)), ((), ())), preferred_element_type=F32)


def _tdot(a, b):
    return lax.dot_general(a, b, (((0,), (0,)), ((), ())), preferred_element_type=F32)


def _split3(a):
    hi = a.astype(BF16)
    r1 = a - hi.astype(F32)
    mid = r1.astype(BF16)
    lo = (r1 - mid.astype(F32)).astype(BF16)
    return hi, mid, lo


def _sel_right(a, m01):
    hi, mid, lo = _split3(a)
    return (_dot(lo, m01) + _dot(mid, m01)) + _dot(hi, m01)


def _sel_left(m01, a):
    hi, mid, lo = _split3(a)
    return (_dot(m01, lo) + _dot(m01, mid)) + _dot(m01, hi)


def _rms(x, g):
    return x * lax.rsqrt(jnp.mean(x * x, axis=-1, keepdims=True) + EPS) * g


def _silu(x):
    return x * jax.nn.sigmoid(x)


def _iota(shape, dim):
    return lax.broadcasted_iota(jnp.int32, shape, dim)


def _mixer_kernel(x_ref, n1g_ref, w_ref, relb_ref, convw_ref, convb_ref, dtb_ref, alog_ref,
                  dskip_ref, normg_ref, dww_ref, dwb_ref, lng_ref, lnb_ref,
                  o_ref,
                  q_s, k_s, v_s, bias_s, z_s, xbc_s, xc_s, glu_s, dte_s, st_s):
    b = pl.program_id(0)
    i = pl.program_id(1)

    @pl.when(jnp.logical_and(b == 0, i == 0))
    def _build_bias():
        rr = _iota((2 * QB, BAND), 0)
        c = _iota((2 * QB, BAND), 1)
        r = jnp.bitwise_and(rr, QB - 1)
        second = rr >= QB
        idx = jnp.clip(LEFT + r - c, -MAX_REL, MAX_REL) + MAX_REL
        qc = jnp.right_shift(r, 6)
        kc = jnp.right_shift(c, 6)
        valid = jnp.logical_and(kc >= qc, kc <= qc + LEFT_CHUNKS)
        for pr in range(2):
            def body(v, acc, pr=pr):
                val = jnp.where(second, relb_ref[2 * pr + 1, v], relb_ref[2 * pr, v])
                return jnp.where(idx == v, val, acc)
            acc = lax.fori_loop(0, N_REL, body, jnp.zeros((2 * QB, BAND), F32))
            bias_s[pr] = jnp.where(valid, acc, NEG_INF)

    @pl.when(i == 0)
    def _reset():
        k_s[:, 0:TS, :] = jnp.zeros((2, TS, 128), BF16)
        v_s[:, 0:TS, :] = jnp.zeros((2, TS, 128), BF16)
        xbc_s[0:XBC_HALO, :] = jnp.zeros((XBC_HALO, XBC_DIM), F32)
        glu_s[0:GLU_HALO, :] = jnp.zeros((GLU_HALO, C_DIM), F32)
        st_s[...] = jnp.zeros((B_GROUPS, B_STATE, 256), F32)

    h = _rms(x_ref[0], n1g_ref[...]).astype(BF16)
    qkv = _dot(h, w_ref[:, OFF_QKV:OFF_Z])
    for pr in range(2):
        q_s[pr] = (qkv[:, pr * 128:(pr + 1) * 128] * 0.125).astype(BF16)
        k_s[pr, TS:2 * TS, :] = qkv[:, 256 + pr * 128:256 + (pr + 1) * 128].astype(BF16)
        v_s[pr, TS:2 * TS, :] = qkv[:, 512 + pr * 128:512 + (pr + 1) * 128].astype(BF16)

    lane = _iota((QB, 128), 1)
    first_head = lane < 64

    def attn_block(j, carry):
        r0 = j * QB
        col_ok = (_iota((1, BAND), 1) + (i * TS + r0 - LEFT)) >= 0
        for pr in range(2):
            qa = q_s[pr, pl.ds(r0, QB), :]
            qq = jnp.concatenate([jnp.where(first_head, qa, jnp.zeros_like(qa)),
                                  jnp.where(first_head, jnp.zeros_like(qa), qa)], axis=0)
            kb = k_s[pr, pl.ds(r0, BAND), :]
            vb = v_s[pr, pl.ds(r0, BAND), :]
            s = _dot_t(qq, kb) + bias_s[pr]
            s = jnp.where(col_ok, s, NEG_INF)
            m = jnp.max(s, axis=-1, keepdims=True)
            p = jnp.exp(s - m)
            l = jnp.sum(p, axis=-1, keepdims=True)
            o = _dot(p.astype(BF16), vb) / l
            o_ref[0, pl.ds(r0, QB), pr * 128:(pr + 1) * 128] = jnp.where(
                first_head, o[:QB], o[QB:]).astype(BF16)
        return carry

    for j in range(TS // QB):
        attn_block(j, 0)

    z_s[...] = _dot(h, w_ref[:, OFF_Z:OFF_XBC])
    xbc_s[XBC_HALO:XBC_HALO + TS, :] = _dot(h, w_ref[:, OFF_XBC:OFF_GLU])
    u = _dot(h, w_ref[:, OFF_GLU:OFF_DT])
    glu_s[GLU_HALO:GLU_HALO + TS, :] = u[:, :C_DIM] * jax.nn.sigmoid(u[:, C_DIM:])
    dt_raw = _dot(h, w_ref[:, OFF_DT:N_PROJ]) + dtb_ref[...]
    dt = jnp.maximum(dt_raw, 0.0) + jnp.log1p(jnp.exp(-jnp.abs(dt_raw)))
    expand = (jnp.right_shift(_iota((DT_PAD, B_DIM), 1), 6) == _iota((DT_PAD, B_DIM), 0)).astype(BF16)
    dte_s[...] = _sel_right(dt, expand)

    for c in range(TS // CHUNK):
        acc = jnp.broadcast_to(convb_ref[...], (CHUNK, XBC_DIM))
        for k in range(B_CONV):
            start = XBC_HALO - (B_CONV - 1) + k + c * CHUNK
            acc = acc + xbc_s[start:start + CHUNK, :] * convw_ref[k:k + 1, :]
        xc_s[c * CHUNK:(c + 1) * CHUNK, :] = _silu(acc)

    a_neg = -jnp.exp(alog_ref[...])
    l_i = _iota((CHUNK, B_DIM), 0)
    s_i = jnp.bitwise_and(_iota((CHUNK, B_DIM), 1), CHUNK - 1)
    causal = l_i >= s_i
    diag = (l_i == s_i).astype(F32)
    tri = (_iota((CHUNK, CHUNK), 0) >= _iota((CHUNK, CHUNK), 1)).astype(BF16)
    ones = jnp.ones((CHUNK, CHUNK), BF16)
    blockdiag = (jnp.right_shift(_iota((256, 256), 0), 6) ==
                 jnp.right_shift(_iota((256, 256), 1), 6)).astype(F32)

    def ssd_chunk(c, carry):
        r0 = c * CHUNK
        xc = xc_s[pl.ds(r0, CHUNK), :]
        xs = xc[:, :B_DIM]
        dte = dte_s[pl.ds(r0, CHUNK), :]
        acs = _sel_left(tri, dte * a_neg)
        acs_row = _sel_left(ones, acs * diag)
        decay = jnp.exp(jnp.where(causal, acs - acs_row, -jnp.inf))
        last = acs[CHUNK - 1:CHUNK, :]
        to_end = jnp.exp(last - acs)
        chunk_decay = jnp.exp(last)
        from_start = jnp.exp(acs)
        xdt = xs * dte
        z = z_s[pl.ds(r0, CHUNK), :]
        for g in range(B_GROUPS):
            cols = slice(g * 256, (g + 1) * 256)
            bg = xc[:, B_DIM + g * B_STATE:B_DIM + (g + 1) * B_STATE].astype(BF16)
            cg = xc[:, B_DIM + B_GROUPS * B_STATE + g * B_STATE:
                    B_DIM + B_GROUPS * B_STATE + (g + 1) * B_STATE].astype(BF16)
            cb = _dot_t(cg, jnp.concatenate([bg] * 4, axis=0))
            scores = (cb * decay[:, cols]).astype(BF16)
            xg = xdt[:, cols]
            x_bd = (jnp.concatenate([xg] * 4, axis=0) * blockdiag).astype(BF16)
            y = _dot(scores, x_bd)
            state = st_s[g]
            y = y + _dot(cg, state.astype(BF16)) * from_start[:, cols]
            st_s[g] = state * chunk_decay[:, cols] + _tdot(bg, (xg * to_end[:, cols]).astype(BF16))
            y = y + xs[:, cols] * dskip_ref[:, cols]
            y = y * _silu(z[:, cols])
            y = y * lax.rsqrt(jnp.mean(y * y, axis=-1, keepdims=True) + EPS) * normg_ref[:, cols]
            o_ref[0, pl.ds(r0, CHUNK), A_DIM + g * 256:A_DIM + (g + 1) * 256] = y.astype(BF16)
        return carry

    for c in range(TS // CHUNK):
        ssd_chunk(c, 0)

    for c in range(TS // CHUNK):
        acc = jnp.broadcast_to(dwb_ref[...], (CHUNK, C_DIM))
        for rho in range(8):
            rows = CHUNK if rho == 0 else CHUNK + 8
            part = None
            for k in range(C_KERNEL):
                off = GLU_HALO - (C_KERNEL - 1) + k
                if off % 8 != rho:
                    continue
                start = c * CHUNK + off - rho
                term = glu_s[start:start + rows, :] * dww_ref[k:k + 1, :]
                part = term if part is None else part + term
            acc = acc + (part if rho == 0 else part[rho:rho + CHUNK, :])
        mu = jnp.mean(acc, axis=-1, keepdims=True)
        d = acc - mu
        var = jnp.mean(d * d, axis=-1, keepdims=True)
        y = d * lax.rsqrt(var + EPS) * lng_ref[...] + lnb_ref[...]
        o_ref[0, c * CHUNK:(c + 1) * CHUNK, A_DIM + B_DIM:D_MIX] = _silu(y).astype(BF16)

    k_s[:, 0:TS, :] = k_s[:, TS:2 * TS, :]
    v_s[:, 0:TS, :] = v_s[:, TS:2 * TS, :]
    xbc_s[0:XBC_HALO, :] = xbc_s[TS:TS + XBC_HALO, :]
    glu_s[0:GLU_HALO, :] = glu_s[TS:TS + GLU_HALO, :]


def _const_spec(shape):
    nd = len(shape)
    return pl.BlockSpec(shape, lambda b, i: (0,) * nd)


def _mixer_call(x, n1g, w_r, relb, convw, convb, dtb, alog, dskip, normg, dww, dwb, lng, lnb):
    bsz, s, _ = x.shape
    in_specs = [
        pl.BlockSpec((1, TS, D_MODEL), lambda b, i: (b, i, 0)),
        _const_spec((1, D_MODEL)),
        _const_spec((D_MODEL, N_PROJ)),
        pl.BlockSpec(memory_space=pltpu.SMEM),
        _const_spec((B_CONV, XBC_DIM)),
        _const_spec((1, XBC_DIM)),
        _const_spec((1, DT_PAD)),
        _const_spec((1, B_DIM)),
        _const_spec((1, B_DIM)),
        _const_spec((1, B_DIM)),
        _const_spec((C_KERNEL, C_DIM)),
        _const_spec((1, C_DIM)),
        _const_spec((1, C_DIM)),
        _const_spec((1, C_DIM)),
    ]
    scratch = [
        pltpu.VMEM((2, TS, 128), BF16),
        pltpu.VMEM((2, 2 * TS, 128), BF16),
        pltpu.VMEM((2, 2 * TS, 128), BF16),
        pltpu.VMEM((2, 2 * QB, BAND), F32),
        pltpu.VMEM((TS, B_DIM), F32),
        pltpu.VMEM((XBC_HALO + TS, XBC_DIM), F32),
        pltpu.VMEM((TS, XBC_DIM), F32),
        pltpu.VMEM((GLU_HALO + TS, C_DIM), F32),
        pltpu.VMEM((TS, B_DIM), F32),
        pltpu.VMEM((B_GROUPS, B_STATE, 256), F32),
    ]
    return pl.pallas_call(
        _mixer_kernel,
        grid=(bsz, s // TS),
        in_specs=in_specs,
        out_specs=pl.BlockSpec((1, TS, D_MIX), lambda b, i: (b, i, 0)),
        out_shape=jax.ShapeDtypeStruct((bsz, s, D_MIX), BF16),
        scratch_shapes=scratch,
        compiler_params=pltpu.CompilerParams(
            dimension_semantics=("arbitrary", "arbitrary"),
            vmem_limit_bytes=VMEM_LIMIT),
        name="mixer",
    )(x, n1g, w_r, relb, convw, convb, dtb, alog, dskip, normg, dww, dwb, lng, lnb)


def _ffn_kernel(x_ref, mix_ref, wout_ref, n2g_ref, wg_ref, wu_ref, wd_ref, fg_ref, o_ref, *, final):
    xn = x_ref[...] + _dot(mix_ref[...], wout_ref[...])
    h = _rms(xn, n2g_ref[...]).astype(BF16)
    acc = xn
    for lo, hi in FF_SLICES:
        g = _dot(h, wg_ref[:, lo:hi])
        u = _dot(h, wu_ref[:, lo:hi])
        acc = acc + _dot((_silu(g) * u).astype(BF16), wd_ref[lo:hi, :])
    if final:
        acc = _rms(acc, fg_ref[...])
    o_ref[...] = acc


def _ffn_call(x2, mix2, wout, n2g, wg, wu, wd, fg, final):
    t = x2.shape[0]

    def resident(shape):
        return pl.BlockSpec(shape, lambda i: (0, 0), pipeline_mode=pl.Buffered(1))

    return pl.pallas_call(
        functools.partial(_ffn_kernel, final=final),
        grid=(t // TM,),
        in_specs=[
            pl.BlockSpec((TM, D_MODEL), lambda i: (i, 0)),
            pl.BlockSpec((TM, D_MIX), lambda i: (i, 0)),
            resident((D_MIX, D_MODEL)),
            resident((1, D_MODEL)),
            resident((D_MODEL, D_FF)),
            resident((D_MODEL, D_FF)),
            resident((D_FF, D_MODEL)),
            resident((1, D_MODEL)),
        ],
        out_specs=pl.BlockSpec((TM, D_MODEL), lambda i: (i, 0)),
        out_shape=jax.ShapeDtypeStruct((t, D_MODEL), F32),
        compiler_params=pltpu.CompilerParams(
            dimension_semantics=("arbitrary",),
            vmem_limit_bytes=VMEM_LIMIT),
        name="ffn_final" if final else "ffn",
    )(x2, mix2, wout, n2g, wg, wu, wd, fg)


def kernel(x, norm1_g, w_in, attn_rel_bias, ssm_conv_w, ssm_conv_b, ssm_dt_bias, ssm_a_log, ssm_d, ssm_norm_g, conv_dw_w, conv_dw_b, conv_ln_g, conv_ln_b, w_out, norm2_g, ffn_w_gate, ffn_w_up, ffn_w_down, final_norm_g):
    bsz, s, d = x.shape
    depth = w_in.shape[0]
    assert d == D_MODEL and s % TS == 0 and (bsz * s) % TM == 0
    row = lambda v: v.reshape(1, -1).astype(F32)
    per_head = lambda v: jnp.repeat(v.astype(F32), B_HEAD_DIM).reshape(1, B_DIM)
    for l in range(depth):
        w_r = jnp.concatenate(
            [w_in[l, :, :OFF_GLU], w_in[l, :, OFF_GLU + B_HEADS:],
             jnp.pad(w_in[l, :, OFF_GLU:OFF_GLU + B_HEADS], ((0, 0), (0, DT_PAD - B_HEADS)))],
            axis=1).astype(BF16)
        dtb = jnp.pad(ssm_dt_bias[l].astype(F32), (0, DT_PAD - B_HEADS)).reshape(1, DT_PAD)
        mix = _mixer_call(
            x, row(norm1_g[l]), w_r, attn_rel_bias[l].astype(F32),
            ssm_conv_w[l].astype(F32), row(ssm_conv_b[l]), dtb,
            per_head(ssm_a_log[l]), per_head(ssm_d[l]), row(ssm_norm_g[l]),
            conv_dw_w[l].astype(F32), row(conv_dw_b[l]), row(conv_ln_g[l]), row(conv_ln_b[l]))
        x = _ffn_call(
            x.reshape(bsz * s, d), mix.reshape(bsz * s, D_MIX),
            w_out[l].astype(BF16), row(norm2_g[l]),
            ffn_w_gate[l].astype(BF16), ffn_w_up[l].astype(BF16), ffn_w_down[l].astype(BF16),
            row(final_norm_g), final=(l == depth - 1)).reshape(bsz, s, d)
    return x
```

```python
import functools

import jax
import jax.numpy as jnp
from jax import lax
from jax.experimental import pallas as pl
from jax.experimental.pallas import tpu as pltpu

F32 = jnp.float32
BF16 = jnp.bfloat16

D_MODEL = 1024
CHUNK = 64
A_HEADS = 4
A_DIM = 256
LEFT_CHUNKS = 8
LEFT = LEFT_CHUNKS * CHUNK
MAX_REL = 128
N_REL = 2 * MAX_REL + 1
B_HEADS = 8
B_HEAD_DIM = 64
B_DIM = 512
B_GROUPS = 2
B_STATE = 128
B_CONV = 4
XBC_DIM = 1024
C_DIM = 256
C_KERNEL = 31
D_MIX = 1024
D_AB = A_DIM + B_DIM
D_FF = 2816
EPS = 1e-6
NEG_INF = -1e30

OFF_QKV = 0
OFF_Z = 768
OFF_XBC = 1280
OFF_GLU = 2304
OFF_DT = 2816
N_PROJ = 2944
DT_PAD = 128

TS = 512
QB = 128
BAND = LEFT + QB
XBC_HALO = 8
GLU_HALO = 32
FF_SLICES = ((0, 512), (512, 1024), (1024, 1536), (1536, 2048), (2048, 2816))
VMEM_LIMIT = 56 * 1024 * 1024


def _dot(a, b):
    return jnp.dot(a, b, preferred_element_type=F32)


def _dot_t(a, b):
    return lax.dot_general(a, b, (((1,), (1,)), ((), ())), preferred_element_type=F32)


def _tdot(a, b):
    return lax.dot_general(a, b, (((0,), (0,)), ((), ())), preferred_element_type=F32)


def _split3(a):
    hi = a.astype(BF16)
    r1 = a - hi.astype(F32)
    mid = r1.astype(BF16)
    lo = (r1 - mid.astype(F32)).astype(BF16)
    return hi, mid, lo


def _sel_right(a, m01):
    hi, mid, lo = _split3(a)
    return (_dot(lo, m01) + _dot(mid, m01)) + _dot(hi, m01)


def _sel_left(m01, a):
    hi, mid, lo = _split3(a)
    return (_dot(m01, lo) + _dot(m01, mid)) + _dot(m01, hi)


def _rms(x, g):
    return x * lax.rsqrt(jnp.mean(x * x, axis=-1, keepdims=True) + EPS) * g


def _silu(x):
    return x * jax.nn.sigmoid(x)


def _iota(shape, dim):
    return lax.broadcasted_iota(jnp.int32, shape, dim)


def _mixer_prep(first, xbc_s, glu_s, dte_s, acs_s, seg_s, convw_ref, convb_ref, dskip_ref,
                dww_ref, dwb_ref, lng_ref, lnb_ref,
                bc_ref, xdt_ref, xdte_ref, scores_ref, fstart_ref, xsd_ref, outc_ref):
    causal = _iota((CHUNK, B_DIM), 0) >= jnp.bitwise_and(_iota((CHUNK, B_DIM), 1), CHUNK - 1)
    all_ones = jnp.ones((C_DIM, C_DIM), BF16)

    def begin():
        xbc_s[0:XBC_HALO, :] = jnp.where(first, 0.0, xbc_s[0:XBC_HALO, :])
        glu_s[0:GLU_HALO, :] = jnp.where(first, 0.0, glu_s[0:GLU_HALO, :])

    def chunk(c):
        r0 = c * CHUNK
        rows = slice(r0, r0 + CHUNK)
        acc = jnp.broadcast_to(convb_ref[...], (CHUNK, XBC_DIM))
        for k in range(B_CONV):
            start = XBC_HALO - (B_CONV - 1) + k + r0
            acc = acc + xbc_s[start:start + CHUNK, :] * convw_ref[k:k + 1, :]
        xc = _silu(acc)
        xs = xc[:, :B_DIM]
        bc = xc[:, B_DIM:].astype(BF16)
        bc_ref[rows, :] = bc
        decay = jnp.exp(jnp.where(causal, seg_s[rows, :], -jnp.inf))
        for g in range(B_GROUPS):
            bg = bc[:, g * B_STATE:(g + 1) * B_STATE]
            cg = bc[:, (B_GROUPS + g) * B_STATE:(B_GROUPS + g + 1) * B_STATE]
            cb = _dot_t(cg, jnp.concatenate([bg] * 4, axis=0))
            scores_ref[rows, g * 256:(g + 1) * 256] = (cb * decay[:, g * 256:(g + 1) * 256]).astype(BF16)
        acs = acs_s[rows, :]
        fstart_ref[rows, :] = jnp.exp(acs)
        xdt = xs * dte_s[rows, :]
        xdt_ref[rows, :] = xdt.astype(BF16)
        xdte_ref[rows, :] = (xdt * jnp.exp(acs[CHUNK - 1:CHUNK, :] - acs)).astype(BF16)
        xsd_ref[rows, :] = (xs * dskip_ref[...]).astype(BF16)
        acc = jnp.broadcast_to(dwb_ref[...], (CHUNK, C_DIM))
        for rho in range(8):
            nrow = CHUNK if rho == 0 else CHUNK + 8
            part = None
            for k in range(C_KERNEL):
                off = GLU_HALO - (C_KERNEL - 1) + k
                if off % 8 != rho:
                    continue
                start = r0 + off - rho
                term = glu_s[start:start + nrow, :] * dww_ref[k:k + 1, :]
                part = term if part is None else part + term
            acc = acc + (part if rho == 0 else part[rho:rho + CHUNK, :])
        hi = acc.astype(BF16)
        mid = (acc - hi.astype(F32)).astype(BF16)
        mu = (_dot(mid, all_ones) + _dot(hi, all_ones)) * (1.0 / C_DIM)
        d = acc - mu
        var = _dot((d * d).astype(BF16), all_ones) * (1.0 / C_DIM)
        y = d * lax.rsqrt(var + EPS) * lng_ref[...] + lnb_ref[...]
        outc_ref[rows, :] = _silu(y).astype(BF16)

    def end():
        xbc_s[0:XBC_HALO, :] = xbc_s[TS:TS + XBC_HALO, :]
        glu_s[0:GLU_HALO, :] = glu_s[TS:TS + GLU_HALO, :]

    return begin, chunk, end


def _dense_kernel(*refs, has_ffn, has_proj, final, tiles_per_seq):
    it = iter(refs)
    take = lambda n: [next(it) for _ in range(n)]
    (x_ref,) = take(1)
    if has_ffn:
        mixab_ref, mixc_ref, wout_ref, n2g_ref, wg_ref, wu_ref, wd_ref = take(7)
    if final:
        (fg_ref,) = take(1)
    if has_proj:
        n1g_ref, win_ref, dtb_ref, alog_ref = take(4)
        prep_params = take(7)
    if has_ffn:
        (xo_ref,) = take(1)
    if has_proj:
        q_ref, kv_ref, zs_ref = take(3)
        prep_outs = take(7)
        scratch = take(5)
        xbc_s, glu_s, dte_s, acs_s, seg_s = scratch

    pending = []
    if has_proj:
        i = pl.program_id(0)

        @pl.when(i == 0)
        def _init():
            for ref in scratch:
                ref[...] = jnp.zeros(ref.shape, F32)

        first = lax.rem(i + (tiles_per_seq - 1), tiles_per_seq) == 0
        prep_begin, prep_chunk, prep_end = _mixer_prep(first, *scratch, *prep_params, *prep_outs)
        prep_begin()
        pending = list(range(TS // CHUNK))

    def prep_some(n):
        for _ in range(min(n, len(pending))):
            prep_chunk(pending.pop(0))

    x = x_ref[...]
    if has_ffn:
        mix = jnp.concatenate([mixab_ref[...], mixc_ref[...]], axis=1)
        xn = x + _dot(mix, wout_ref[...])
        h = _rms(xn, n2g_ref[...]).astype(BF16)
        x = xn
        for lo, hi in FF_SLICES:
            g = _dot(h, wg_ref[:, lo:hi])
            u = _dot(h, wu_ref[:, lo:hi])
            x = x + _dot((_silu(g) * u).astype(BF16), wd_ref[lo:hi, :])
            prep_some(2)
        xo_ref[...] = _rms(x, fg_ref[...]) if final else x
    if has_proj:
        h = _rms(x, n1g_ref[...]).astype(BF16)
        dt_raw = _dot(h, win_ref[:, OFF_DT:N_PROJ]) + dtb_ref[...]
        dt = jnp.maximum(dt_raw, 0.0) + jnp.log1p(jnp.exp(-jnp.abs(dt_raw)))
        prep_some(len(pending))
        prep_end()
        qkv = _dot(h, win_ref[:, OFF_QKV:OFF_Z])
        q_ref[...] = (qkv[:, :A_DIM] * 0.125).astype(BF16)
        kv_ref[...] = qkv[:, A_DIM:].astype(BF16)
        expand = (jnp.right_shift(_iota((DT_PAD, B_DIM), 1), 6) ==
                  _iota((DT_PAD, B_DIM), 0)).astype(BF16)
        dte_s[...] = _sel_right(dt, expand)
        zs_ref[...] = _silu(_dot(h, win_ref[:, OFF_Z:OFF_XBC])).astype(BF16)
        a_neg = -jnp.exp(alog_ref[...])
        l_i = _iota((CHUNK, B_DIM), 0)
        s_i = jnp.bitwise_and(_iota((CHUNK, B_DIM), 1), CHUNK - 1)
        diag = (l_i == s_i).astype(F32)
        tri = (_iota((CHUNK, CHUNK), 0) >= _iota((CHUNK, CHUNK), 1)).astype(BF16)
        ones = jnp.ones((CHUNK, CHUNK), BF16)
        chunk_rows = [slice(c * CHUNK, (c + 1) * CHUNK) for c in range(TS // CHUNK)]
        for rows in chunk_rows:
            acs_s[rows, :] = _sel_left(tri, dte_s[rows, :] * a_neg)
        xbc_s[XBC_HALO:XBC_HALO + TS, :] = _dot(h, win_ref[:, OFF_XBC:OFF_GLU])
        for rows in chunk_rows:
            acs_c = acs_s[rows, :]
            seg_s[rows, :] = acs_c - _sel_left(ones, acs_c * diag)
        u = _dot(h, win_ref[:, OFF_GLU:OFF_DT])
        glu_s[GLU_HALO:GLU_HALO + TS, :] = u[:, :C_DIM] * jax.nn.sigmoid(u[:, C_DIM:])


def _dense_call(x2, tiles_per_seq, ffn=None, final_g=None, proj=None, prep=None):
    t = x2.shape[0]
    n_tiles = t // TS
    has_ffn, has_proj, final = ffn is not None, proj is not None, final_g is not None
    last = n_tiles - 1

    def now(c):
        return pl.BlockSpec((TS, c), lambda i: (jnp.minimum(i, last), 0))

    def before(c):
        return pl.BlockSpec((TS, c), lambda i: (jnp.maximum(i - 1, 0), 0))

    def resident(a):
        return pl.BlockSpec(a.shape, lambda i: (0, 0), pipeline_mode=pl.Buffered(1))

    args, in_specs, out_shape, out_specs, scratch = [x2], [now(D_MODEL)], [], [], []
    if has_ffn:
        args += list(ffn)
        in_specs += [now(D_AB), now(C_DIM)] + [resident(a) for a in ffn[2:]]
        out_shape.append(jax.ShapeDtypeStruct((t, D_MODEL), F32))
        out_specs.append(now(D_MODEL))
    if final:
        args.append(final_g)
        in_specs.append(resident(final_g))
    if has_proj:
        args += list(proj) + list(prep)
        in_specs += [resident(a) for a in list(proj) + list(prep)]
        for c, dt in ((A_DIM, BF16), (2 * A_DIM, BF16), (B_DIM, BF16)):
            out_shape.append(jax.ShapeDtypeStruct((t, c), dt))
            out_specs.append(now(c))
        for c, dt in ((B_DIM, BF16), (B_DIM, BF16), (B_DIM, BF16), (B_DIM, BF16), (B_DIM, F32),
                      (B_DIM, BF16), (C_DIM, BF16)):
            out_shape.append(jax.ShapeDtypeStruct((t, c), dt))
            out_specs.append(before(c))
        scratch = [pltpu.VMEM((XBC_HALO + TS, XBC_DIM), F32),
                   pltpu.VMEM((GLU_HALO + TS, C_DIM), F32),
                   pltpu.VMEM((TS, B_DIM), F32),
                   pltpu.VMEM((TS, B_DIM), F32),
                   pltpu.VMEM((TS, B_DIM), F32)]
    return pl.pallas_call(
        functools.partial(_dense_kernel, has_ffn=has_ffn, has_proj=has_proj, final=final,
                          tiles_per_seq=tiles_per_seq),
        grid=(n_tiles + (1 if has_proj else 0),),
        in_specs=in_specs,
        out_specs=out_specs,
        out_shape=out_shape,
        scratch_shapes=scratch,
        compiler_params=pltpu.CompilerParams(
            dimension_semantics=("arbitrary",),
            vmem_limit_bytes=VMEM_LIMIT),
        name="dense_" + ("f" if has_ffn else "") + ("p" if has_proj else "") + ("n" if final else ""),
    )(*args)


def _mixer_kernel(q_ref, kv_ref, kvp_ref, zs_ref, bc_ref, xdt_ref, xdte_ref, scores_ref, fstart_ref,
                  xsd_ref, relb_ref, normg_ref,
                  o_ref,
                  bias_s, st_s):
    b = pl.program_id(0)
    i = pl.program_id(1)

    @pl.when(jnp.logical_and(b == 0, i == 0))
    def _build_bias():
        rr = _iota((2 * QB, BAND), 0)
        c = _iota((2 * QB, BAND), 1)
        r = jnp.bitwise_and(rr, QB - 1)
        second = rr >= QB
        idx = jnp.clip(LEFT + r - c, -MAX_REL, MAX_REL) + MAX_REL
        qc = jnp.right_shift(r, 6)
        kc = jnp.right_shift(c, 6)
        valid = jnp.logical_and(kc >= qc, kc <= qc + LEFT_CHUNKS)
        for pr in range(2):
            def body(v, acc, pr=pr):
                val = jnp.where(second, relb_ref[2 * pr + 1, v], relb_ref[2 * pr, v])
                return jnp.where(idx == v, val, acc)
            acc = lax.fori_loop(0, N_REL, body, jnp.zeros((2 * QB, BAND), F32))
            bias_s[pr] = jnp.where(valid, acc, NEG_INF)

    @pl.when(i == 0)
    def _reset():
        st_s[...] = jnp.zeros((B_GROUPS, B_STATE, 256), F32)

    lane = _iota((QB, 128), 1)
    first_head = lane < 64
    def attn_scores(j, pr):
        r0 = j * QB
        qa = q_ref[0, r0:r0 + QB, pr * 128:(pr + 1) * 128]
        qq = jnp.concatenate([jnp.where(first_head, qa, jnp.zeros_like(qa)),
                              jnp.where(first_head, jnp.zeros_like(qa), qa)], axis=0)
        kcol = slice(pr * 128, (pr + 1) * 128)
        kb = jnp.concatenate([kvp_ref[0, r0:TS, kcol], kv_ref[0, 0:r0 + QB, kcol]], axis=0)
        return _dot_t(qq, kb)

    def attn_finish(j, pr, s):
        r0 = j * QB
        col_ok = (_iota((1, BAND), 1) + (i * TS + r0 - LEFT)) >= 0
        s = jnp.where(col_ok, s + bias_s[pr], NEG_INF)
        m = jnp.max(s, axis=-1, keepdims=True)
        p = jnp.exp(s - m)
        l = jnp.sum(p, axis=-1, keepdims=True)
        vcol = slice(A_DIM + pr * 128, A_DIM + (pr + 1) * 128)
        vb = jnp.concatenate([kvp_ref[0, r0:TS, vcol], kv_ref[0, 0:r0 + QB, vcol]], axis=0)
        o = _dot(p.astype(BF16), vb) / l
        o_ref[0, r0:r0 + QB, pr * 128:(pr + 1) * 128] = jnp.where(
            first_head, o[:QB], o[QB:]).astype(BF16)

    head_of_lane = jnp.right_shift(_iota((1, 256), 1), 6)
    head_mask = [(head_of_lane == hh).astype(BF16) for hh in range(4)]

    def ssd_stateless(c):
        rows = slice(c * CHUNK, (c + 1) * CHUNK)
        out = []
        for g in range(B_GROUPS):
            cols = slice(g * 256, (g + 1) * 256)
            bg = bc_ref[0, rows, g * B_STATE:(g + 1) * B_STATE]
            xg = xdt_ref[0, rows, cols]
            x_bd = jnp.concatenate([xg * head_mask[hh] for hh in range(4)], axis=0)
            out.append((_dot(scores_ref[0, rows, cols], x_bd), _tdot(bg, xdte_ref[0, rows, cols])))
        return out

    def ssd_scan(c, stateless):
        r0 = c * CHUNK
        rows = slice(r0, r0 + CHUNK)
        for g in range(B_GROUPS):
            cols = slice(g * 256, (g + 1) * 256)
            y_intra, state_in = stateless[g]
            cg = bc_ref[0, rows, (B_GROUPS + g) * B_STATE:(B_GROUPS + g + 1) * B_STATE]
            state = st_s[g]
            y = y_intra + _dot(cg, state.astype(BF16)) * fstart_ref[0, rows, cols]
            chunk_decay = fstart_ref[0, r0 + CHUNK - 1:r0 + CHUNK, cols]
            st_s[g] = state * chunk_decay + state_in
            y = (y + xsd_ref[0, rows, cols]) * zs_ref[0, rows, cols]
            y = y * lax.rsqrt(jnp.mean(y * y, axis=-1, keepdims=True) + EPS) * normg_ref[:, cols]
            o_ref[0, rows, A_DIM + g * 256:A_DIM + (g + 1) * 256] = y.astype(BF16)

    units = [(j, pr) for j in range(TS // QB) for pr in range(2)]
    n_chunks = TS // CHUNK
    assert len(units) == n_chunks
    s_next = attn_scores(*units[0])
    stateless_next = ssd_stateless(0)
    for k in range(n_chunks):
        s_cur, stateless_cur = s_next, stateless_next
        if k + 1 < n_chunks:
            s_next = attn_scores(*units[k + 1])
        attn_finish(*units[k], s_cur)
        if k + 1 < n_chunks:
            stateless_next = ssd_stateless(k + 1)
        ssd_scan(k, stateless_cur)


def _mixer_call(q, kv, zs, bc, xdt, xdte, scores, fstart, xsd, relb, normg):
    bsz, s, _ = q.shape

    def tile(c):
        return pl.BlockSpec((1, TS, c), lambda b, i: (b, i, 0))

    in_specs = [
        tile(A_DIM), tile(2 * A_DIM),
        pl.BlockSpec((1, TS, 2 * A_DIM), lambda b, i: (b, jnp.maximum(i - 1, 0), 0)),
        tile(B_DIM), tile(B_DIM), tile(B_DIM), tile(B_DIM), tile(B_DIM), tile(B_DIM), tile(B_DIM),
        pl.BlockSpec(memory_space=pltpu.SMEM),
        pl.BlockSpec(normg.shape, lambda b, i: (0, 0)),
    ]
    scratch = [
        pltpu.VMEM((2, 2 * QB, BAND), F32),
        pltpu.VMEM((B_GROUPS, B_STATE, 256), F32),
    ]
    return pl.pallas_call(
        _mixer_kernel,
        grid=(bsz, s // TS),
        in_specs=in_specs,
        out_specs=pl.BlockSpec((1, TS, D_AB), lambda b, i: (b, i, 0)),
        out_shape=jax.ShapeDtypeStruct((bsz, s, D_AB), BF16),
        scratch_shapes=scratch,
        compiler_params=pltpu.CompilerParams(
            dimension_semantics=("arbitrary", "arbitrary"),
            vmem_limit_bytes=VMEM_LIMIT),
        name="mixer",
    )(q, kv, kv, zs, bc, xdt, xdte, scores, fstart, xsd, relb, normg)


def kernel(x, norm1_g, w_in, attn_rel_bias, ssm_conv_w, ssm_conv_b, ssm_dt_bias, ssm_a_log, ssm_d, ssm_norm_g, conv_dw_w, conv_dw_b, conv_ln_g, conv_ln_b, w_out, norm2_g, ffn_w_gate, ffn_w_up, ffn_w_down, final_norm_g):
    bsz, s, d = x.shape
    depth = w_in.shape[0]
    t = bsz * s
    assert d == D_MODEL and s % TS == 0
    tiles_per_seq = s // TS
    row = lambda v: v.reshape(1, -1).astype(F32)
    per_head = lambda v: jnp.repeat(v.astype(F32), B_HEAD_DIM).reshape(1, B_DIM)

    def proj_args(l):
        w_r = jnp.concatenate(
            [w_in[l, :, :OFF_GLU], w_in[l, :, OFF_GLU + B_HEADS:],
             jnp.pad(w_in[l, :, OFF_GLU:OFF_GLU + B_HEADS], ((0, 0), (0, DT_PAD - B_HEADS)))],
            axis=1).astype(BF16)
        dtb = jnp.pad(ssm_dt_bias[l].astype(F32), (0, DT_PAD - B_HEADS)).reshape(1, DT_PAD)
        return dict(
            proj=(row(norm1_g[l]), w_r, dtb, per_head(ssm_a_log[l])),
            prep=(ssm_conv_w[l].astype(F32), row(ssm_conv_b[l]),
                  per_head(ssm_d[l]), conv_dw_w[l].astype(F32), row(conv_dw_b[l]),
                  row(conv_ln_g[l]), row(conv_ln_b[l])))

    def ffn_args(l, mix_ab, mix_c):
        return (mix_ab.reshape(t, D_AB), mix_c, w_out[l].astype(BF16), row(norm2_g[l]),
                ffn_w_gate[l].astype(BF16), ffn_w_up[l].astype(BF16), ffn_w_down[l].astype(BF16))

    x2 = x.reshape(t, d)
    acts = _dense_call(x2, tiles_per_seq, **proj_args(0))
    for l in range(depth):
        *seq_acts, mix_c = acts
        q, kv, zs, bc, xdt, xdte, scores, fstart, xsd = (a.reshape(bsz, s, -1) for a in seq_acts)
        mix_ab = _mixer_call(q, kv, zs, bc, xdt, xdte, scores, fstart, xsd,
                             attn_rel_bias[l].astype(F32), row(ssm_norm_g[l]))
        if l + 1 < depth:
            x2, *acts = _dense_call(x2, tiles_per_seq, ffn=ffn_args(l, mix_ab, mix_c),
                                    **proj_args(l + 1))
        else:
            (x2,) = _dense_call(x2, tiles_per_seq, ffn=ffn_args(l, mix_ab, mix_c),
                                final_g=row(final_norm_g))
    return x2.reshape(bsz, s, d)
```

```python
import functools

import jax
import jax.numpy as jnp
from jax import lax
from jax.experimental import pallas as pl
from jax.experimental.pallas import tpu as pltpu

F32 = jnp.float32
BF16 = jnp.bfloat16

D_MODEL = 1024
CHUNK = 64
A_HEADS = 4
A_DIM = 256
LEFT_CHUNKS = 8
LEFT = LEFT_CHUNKS * CHUNK
MAX_REL = 128
N_REL = 2 * MAX_REL + 1
B_HEADS = 8
B_HEAD_DIM = 64
B_DIM = 512
B_GROUPS = 2
B_STATE = 128
B_CONV = 4
XBC_DIM = 1024
C_DIM = 256
C_KERNEL = 31
D_MIX = 1024
D_AB = A_DIM + B_DIM
D_FF = 2816
EPS = 1e-6
NEG_INF = -1e30

OFF_QKV = 0
OFF_Z = 768
OFF_XBC = 1280
OFF_GLU = 2304
OFF_DT = 2816
N_PROJ = 2944
DT_PAD = 128

TS = 512
QB = 128
BAND = LEFT + QB
XBC_HALO = 8
GLU_HALO = 32
FF_SLICES = ((0, 512), (512, 1024), (1024, 1536), (1536, 2048), (2048, 2816))
VMEM_LIMIT = 56 * 1024 * 1024


def _dot(a, b):
    return jnp.dot(a, b, preferred_element_type=F32)


def _dot_t(a, b):
    return lax.dot_general(a, b, (((1,), (1,)), ((), ())), preferred_element_type=F32)


def _tdot(a, b):
    return lax.dot_general(a, b, (((0,), (0,)), ((), ())), preferred_element_type=F32)


def _split3(a):
    hi = a.astype(BF16)
    r1 = a - hi.astype(F32)
    mid = r1.astype(BF16)
    lo = (r1 - mid.astype(F32)).astype(BF16)
    return hi, mid, lo


def _sel_right(a, m01):
    hi, mid, lo = _split3(a)
    return (_dot(lo, m01) + _dot(mid, m01)) + _dot(hi, m01)


def _sel_left(m01, a):
    hi, mid, lo = _split3(a)
    return (_dot(m01, lo) + _dot(m01, mid)) + _dot(m01, hi)


def _rms(x, g):
    return x * lax.rsqrt(jnp.mean(x * x, axis=-1, keepdims=True) + EPS) * g


def _silu(x):
    return x * jax.nn.sigmoid(x)


def _iota(shape, dim):
    return lax.broadcasted_iota(jnp.int32, shape, dim)


def _mixer_prep(first, xbc_s, glu_s, dte_s, acs_s, seg_s, convw_ref, convb_ref, dskip_ref,
                dww_ref, dwb_ref, lng_ref, lnb_ref,
                bc_ref, xdt_ref, xdte_ref, scores_ref, fstart_ref, xsd_ref, outc_ref):
    causal = _iota((CHUNK, B_DIM), 0) >= jnp.bitwise_and(_iota((CHUNK, B_DIM), 1), CHUNK - 1)
    all_ones = jnp.ones((C_DIM, C_DIM), BF16)

    def begin():
        xbc_s[0:XBC_HALO, :] = jnp.where(first, 0.0, xbc_s[0:XBC_HALO, :])
        glu_s[0:GLU_HALO, :] = jnp.where(first, 0.0, glu_s[0:GLU_HALO, :])

    def chunk(c):
        r0 = c * CHUNK
        rows = slice(r0, r0 + CHUNK)
        acc = jnp.broadcast_to(convb_ref[...], (CHUNK, XBC_DIM))
        for k in range(B_CONV):
            start = XBC_HALO - (B_CONV - 1) + k + r0
            acc = acc + xbc_s[start:start + CHUNK, :] * convw_ref[k:k + 1, :]
        xc = _silu(acc)
        xs = xc[:, :B_DIM]
        bc = xc[:, B_DIM:].astype(BF16)
        bc_ref[rows, :] = bc
        decay = jnp.exp(jnp.where(causal, seg_s[rows, :], -jnp.inf))
        for g in range(B_GROUPS):
            bg = bc[:, g * B_STATE:(g + 1) * B_STATE]
            cg = bc[:, (B_GROUPS + g) * B_STATE:(B_GROUPS + g + 1) * B_STATE]
            cb = _dot_t(cg, jnp.concatenate([bg] * 4, axis=0))
            scores_ref[rows, g * 256:(g + 1) * 256] = (cb * decay[:, g * 256:(g + 1) * 256]).astype(BF16)
        acs = acs_s[rows, :]
        fstart_ref[rows, :] = jnp.exp(acs)
        xdt = xs * dte_s[rows, :]
        xdt_ref[rows, :] = xdt.astype(BF16)
        xdte_ref[rows, :] = (xdt * jnp.exp(acs[CHUNK - 1:CHUNK, :] - acs)).astype(BF16)
        xsd_ref[rows, :] = (xs * dskip_ref[...]).astype(BF16)
        acc = jnp.broadcast_to(dwb_ref[...], (CHUNK, C_DIM))
        for rho in range(8):
            nrow = CHUNK if rho == 0 else CHUNK + 8
            part = None
            for k in range(C_KERNEL):
                off = GLU_HALO - (C_KERNEL - 1) + k
                if off % 8 != rho:
                    continue
                start = r0 + off - rho
                term = glu_s[start:start + nrow, :] * dww_ref[k:k + 1, :]
                part = term if part is None else part + term
            acc = acc + (part if rho == 0 else part[rho:rho + CHUNK, :])
        hi = acc.astype(BF16)
        mid = (acc - hi.astype(F32)).astype(BF16)
        mu = (_dot(mid, all_ones) + _dot(hi, all_ones)) * (1.0 / C_DIM)
        d = acc - mu
        var = _dot((d * d).astype(BF16), all_ones) * (1.0 / C_DIM)
        y = d * lax.rsqrt(var + EPS) * lng_ref[...] + lnb_ref[...]
        outc_ref[rows, :] = _silu(y).astype(BF16)

    def end():
        xbc_s[0:XBC_HALO, :] = xbc_s[TS:TS + XBC_HALO, :]
        glu_s[0:GLU_HALO, :] = glu_s[TS:TS + GLU_HALO, :]

    return begin, chunk, end


def _dense_kernel(*refs, has_ffn, has_proj, final, tiles_per_seq):
    it = iter(refs)
    take = lambda n: [next(it) for _ in range(n)]
    (x_ref,) = take(1)
    if has_ffn:
        mixab_ref, mixc_ref, wout_ref, n2g_ref, wg_ref, wu_ref, wd_ref = take(7)
    if final:
        (fg_ref,) = take(1)
    if has_proj:
        n1g_ref, win_ref, dtb_ref, alog_ref = take(4)
        prep_params = take(7)
    if has_ffn:
        (xo_ref,) = take(1)
    if has_proj:
        q_ref, kv_ref, zs_ref = take(3)
        prep_outs = take(7)
        scratch = take(5)
        xbc_s, glu_s, dte_s, acs_s, seg_s = scratch

    pending = []
    if has_proj:
        i = pl.program_id(0)

        @pl.when(i == 0)
        def _init():
            for ref in scratch:
                ref[...] = jnp.zeros(ref.shape, F32)

        first = lax.rem(i + (tiles_per_seq - 1), tiles_per_seq) == 0
        prep_begin, prep_chunk, prep_end = _mixer_prep(first, *scratch, *prep_params, *prep_outs)
        prep_begin()
        pending = list(range(TS // CHUNK))

    def prep_some(n):
        for _ in range(min(n, len(pending))):
            prep_chunk(pending.pop(0))

    x = x_ref[...]
    if has_ffn:
        mix = jnp.concatenate([mixab_ref[...], mixc_ref[...]], axis=1)
        xn = x + _dot(mix, wout_ref[...])
        h = _rms(xn, n2g_ref[...]).astype(BF16)
        x = xn
        for lo, hi in FF_SLICES:
            g = _dot(h, wg_ref[:, lo:hi])
            u = _dot(h, wu_ref[:, lo:hi])
            x = x + _dot((_silu(g) * u).astype(BF16), wd_ref[lo:hi, :])
            prep_some(2)
        xo_ref[...] = _rms(x, fg_ref[...]) if final else x
    if has_proj:
        h = _rms(x, n1g_ref[...]).astype(BF16)
        dt_raw = _dot(h, win_ref[:, OFF_DT:N_PROJ]) + dtb_ref[...]
        dt = jnp.maximum(dt_raw, 0.0) + jnp.log1p(jnp.exp(-jnp.abs(dt_raw)))
        prep_some(len(pending))
        prep_end()
        qkv = _dot(h, win_ref[:, OFF_QKV:OFF_Z])
        q_ref[...] = (qkv[:, :A_DIM] * 0.125).astype(BF16)
        kv_ref[...] = qkv[:, A_DIM:].astype(BF16)
        expand = (jnp.right_shift(_iota((DT_PAD, B_DIM), 1), 6) ==
                  _iota((DT_PAD, B_DIM), 0)).astype(BF16)
        dte_s[...] = _sel_right(dt, expand)
        zs_ref[...] = _silu(_dot(h, win_ref[:, OFF_Z:OFF_XBC])).astype(BF16)
        a_neg = -jnp.exp(alog_ref[...])
        l_i = _iota((CHUNK, B_DIM), 0)
        s_i = jnp.bitwise_and(_iota((CHUNK, B_DIM), 1), CHUNK - 1)
        diag = (l_i == s_i).astype(F32)
        tri = (_iota((CHUNK, CHUNK), 0) >= _iota((CHUNK, CHUNK), 1)).astype(BF16)
        ones = jnp.ones((CHUNK, CHUNK), BF16)
        chunk_rows = [slice(c * CHUNK, (c + 1) * CHUNK) for c in range(TS // CHUNK)]
        for rows in chunk_rows:
            acs_s[rows, :] = _sel_left(tri, dte_s[rows, :] * a_neg)
        xbc_s[XBC_HALO:XBC_HALO + TS, :] = _dot(h, win_ref[:, OFF_XBC:OFF_GLU])
        for rows in chunk_rows:
            acs_c = acs_s[rows, :]
            seg_s[rows, :] = acs_c - _sel_left(ones, acs_c * diag)
        u = _dot(h, win_ref[:, OFF_GLU:OFF_DT])
        glu_s[GLU_HALO:GLU_HALO + TS, :] = u[:, :C_DIM] * jax.nn.sigmoid(u[:, C_DIM:])


def _dense_call(x2, tiles_per_seq, ffn=None, final_g=None, proj=None, prep=None):
    t = x2.shape[0]
    n_tiles = t // TS
    has_ffn, has_proj, final = ffn is not None, proj is not None, final_g is not None
    last = n_tiles - 1

    def now(c):
        return pl.BlockSpec((TS, c), lambda i: (jnp.minimum(i, last), 0))

    def before(c):
        return pl.BlockSpec((TS, c), lambda i: (jnp.maximum(i - 1, 0), 0))

    def resident(a):
        return pl.BlockSpec(a.shape, lambda i: (0, 0), pipeline_mode=pl.Buffered(1))

    args, in_specs, out_shape, out_specs, scratch = [x2], [now(D_MODEL)], [], [], []
    if has_ffn:
        args += list(ffn)
        in_specs += [now(D_AB), now(C_DIM)] + [resident(a) for a in ffn[2:]]
        out_shape.append(jax.ShapeDtypeStruct((t, D_MODEL), F32))
        out_specs.append(now(D_MODEL))
    if final:
        args.append(final_g)
        in_specs.append(resident(final_g))
    if has_proj:
        args += list(proj) + list(prep)
        in_specs += [resident(a) for a in list(proj) + list(prep)]
        for c, dt in ((A_DIM, BF16), (2 * A_DIM, BF16), (B_DIM, BF16)):
            out_shape.append(jax.ShapeDtypeStruct((t, c), dt))
            out_specs.append(now(c))
        for c, dt in ((B_DIM, BF16), (B_DIM, BF16), (B_DIM, BF16), (B_DIM, BF16), (B_DIM, F32),
                      (B_DIM, BF16), (C_DIM, BF16)):
            out_shape.append(jax.ShapeDtypeStruct((t, c), dt))
            out_specs.append(before(c))
        scratch = [pltpu.VMEM((XBC_HALO + TS, XBC_DIM), F32),
                   pltpu.VMEM((GLU_HALO + TS, C_DIM), F32),
                   pltpu.VMEM((TS, B_DIM), F32),
                   pltpu.VMEM((TS, B_DIM), F32),
                   pltpu.VMEM((TS, B_DIM), F32)]
    return pl.pallas_call(
        functools.partial(_dense_kernel, has_ffn=has_ffn, has_proj=has_proj, final=final,
                          tiles_per_seq=tiles_per_seq),
        grid=(n_tiles + (1 if has_proj else 0),),
        in_specs=in_specs,
        out_specs=out_specs,
        out_shape=out_shape,
        scratch_shapes=scratch,
        compiler_params=pltpu.CompilerParams(
            dimension_semantics=("arbitrary",),
            vmem_limit_bytes=VMEM_LIMIT,
            ),
        name="dense_" + ("f" if has_ffn else "") + ("p" if has_proj else "") + ("n" if final else ""),
    )(*args)


def _mixer_kernel(q_ref, kv_ref, kvp_ref, zs_ref, bc_ref, xdt_ref, xdte_ref, scores_ref, fstart_ref,
                  xsd_ref, relb_ref, normg_ref,
                  o_ref,
                  bias_s, st_s):
    b = pl.program_id(0)
    i = pl.program_id(1)

    @pl.when(jnp.logical_and(b == 0, i == 0))
    def _build_bias():
        gvw = BAND + QB
        d = _iota((8, gvw), 1) - _iota((8, gvw), 0) - (QB - 1)
        idx = jnp.clip(LEFT - d, -MAX_REL, MAX_REL) + MAX_REL
        rr = _iota((2 * QB, BAND), 0)
        c = _iota((2 * QB, BAND), 1)
        qc = jnp.right_shift(jnp.bitwise_and(rr, QB - 1), 6)
        kc = jnp.right_shift(c, 6)
        valid = jnp.logical_and(kc >= qc, kc <= qc + LEFT_CHUNKS)
        for pr in range(2):
            halves = []
            for head in (2 * pr, 2 * pr + 1):
                gv8 = lax.fori_loop(
                    0, N_REL, lambda v, acc, head=head: jnp.where(idx == v, relb_ref[head, v], acc),
                    jnp.zeros((8, gvw), F32))
                halves += [gv8[:, QB - 1 - 8 * blk:QB - 1 - 8 * blk + BAND] for blk in range(QB // 8)]
            bias_s[pr] = jnp.where(valid, jnp.concatenate(halves, axis=0), NEG_INF)

    @pl.when(i == 0)
    def _reset():
        st_s[...] = jnp.zeros((B_GROUPS, B_STATE, 256), F32)

    lane = _iota((QB, 128), 1)
    first_head = lane < 64
    def attn_scores(j, pr):
        r0 = j * QB
        qa = q_ref[0, r0:r0 + QB, pr * 128:(pr + 1) * 128]
        qq = jnp.concatenate([jnp.where(first_head, qa, jnp.zeros_like(qa)),
                              jnp.where(first_head, jnp.zeros_like(qa), qa)], axis=0)
        kcol = slice(pr * 128, (pr + 1) * 128)
        kb = jnp.concatenate([kvp_ref[0, r0:TS, kcol], kv_ref[0, 0:r0 + QB, kcol]], axis=0)
        return _dot_t(qq, kb)

    def attn_finish(j, pr, s):
        r0 = j * QB
        col_ok = (_iota((1, BAND), 1) + (i * TS + r0 - LEFT)) >= 0
        s = jnp.where(col_ok, s + bias_s[pr], NEG_INF)
        m = jnp.max(s, axis=-1, keepdims=True)
        p = jnp.exp(s - m)
        l = jnp.sum(p, axis=-1, keepdims=True)
        vcol = slice(A_DIM + pr * 128, A_DIM + (pr + 1) * 128)
        vb = jnp.concatenate([kvp_ref[0, r0:TS, vcol], kv_ref[0, 0:r0 + QB, vcol]], axis=0)
        o = _dot(p.astype(BF16), vb) / l
        o_ref[0, r0:r0 + QB, pr * 128:(pr + 1) * 128] = jnp.where(
            first_head, o[:QB], o[QB:]).astype(BF16)

    head_of_lane = jnp.right_shift(_iota((1, 256), 1), 6)
    head_mask = [(head_of_lane == hh).astype(BF16) for hh in range(4)]

    def ssd_stateless(c):
        rows = slice(c * CHUNK, (c + 1) * CHUNK)
        out = []
        for g in range(B_GROUPS):
            cols = slice(g * 256, (g + 1) * 256)
            bg = bc_ref[0, rows, g * B_STATE:(g + 1) * B_STATE]
            xg = xdt_ref[0, rows, cols]
            x_bd = jnp.concatenate([xg * head_mask[hh] for hh in range(4)], axis=0)
            out.append((_dot(scores_ref[0, rows, cols], x_bd), _tdot(bg, xdte_ref[0, rows, cols])))
        return out

    def ssd_scan(c, stateless):
        r0 = c * CHUNK
        rows = slice(r0, r0 + CHUNK)
        for g in range(B_GROUPS):
            cols = slice(g * 256, (g + 1) * 256)
            y_intra, state_in = stateless[g]
            cg = bc_ref[0, rows, (B_GROUPS + g) * B_STATE:(B_GROUPS + g + 1) * B_STATE]
            state = st_s[g]
            y = y_intra + _dot(cg, state.astype(BF16)) * fstart_ref[0, rows, cols]
            chunk_decay = fstart_ref[0, r0 + CHUNK - 1:r0 + CHUNK, cols]
            st_s[g] = state * chunk_decay + state_in
            y = (y + xsd_ref[0, rows, cols]) * zs_ref[0, rows, cols]
            y = y * lax.rsqrt(jnp.mean(y * y, axis=-1, keepdims=True) + EPS) * normg_ref[:, cols]
            o_ref[0, rows, A_DIM + g * 256:A_DIM + (g + 1) * 256] = y.astype(BF16)

    units = [(j, pr) for j in range(TS // QB) for pr in range(2)]
    n_chunks = TS // CHUNK
    assert len(units) == n_chunks
    s_next = attn_scores(*units[0])
    stateless_next = ssd_stateless(0)
    for k in range(n_chunks):
        s_cur, stateless_cur = s_next, stateless_next
        if k + 1 < n_chunks:
            s_next = attn_scores(*units[k + 1])
        attn_finish(*units[k], s_cur)
        if k + 1 < n_chunks:
            stateless_next = ssd_stateless(k + 1)
        ssd_scan(k, stateless_cur)


def _mixer_call(q, kv, zs, bc, xdt, xdte, scores, fstart, xsd, relb, normg):
    bsz, s, _ = q.shape

    def tile(c):
        return pl.BlockSpec((1, TS, c), lambda b, i: (b, i, 0))

    in_specs = [
        tile(A_DIM), tile(2 * A_DIM),
        pl.BlockSpec((1, TS, 2 * A_DIM), lambda b, i: (b, jnp.maximum(i - 1, 0), 0)),
        tile(B_DIM), tile(B_DIM), tile(B_DIM), tile(B_DIM), tile(B_DIM), tile(B_DIM), tile(B_DIM),
        pl.BlockSpec(memory_space=pltpu.SMEM),
        pl.BlockSpec(normg.shape, lambda b, i: (0, 0)),
    ]
    scratch = [
        pltpu.VMEM((2, 2 * QB, BAND), F32),
        pltpu.VMEM((B_GROUPS, B_STATE, 256), F32),
    ]
    return pl.pallas_call(
        _mixer_kernel,
        grid=(bsz, s // TS),
        in_specs=in_specs,
        out_specs=pl.BlockSpec((1, TS, D_AB), lambda b, i: (b, i, 0)),
        out_shape=jax.ShapeDtypeStruct((bsz, s, D_AB), BF16),
        scratch_shapes=scratch,
        compiler_params=pltpu.CompilerParams(
            dimension_semantics=("arbitrary", "arbitrary"),
            vmem_limit_bytes=VMEM_LIMIT),
        name="mixer",
    )(q, kv, kv, zs, bc, xdt, xdte, scores, fstart, xsd, relb, normg)


def kernel(x, norm1_g, w_in, attn_rel_bias, ssm_conv_w, ssm_conv_b, ssm_dt_bias, ssm_a_log, ssm_d, ssm_norm_g, conv_dw_w, conv_dw_b, conv_ln_g, conv_ln_b, w_out, norm2_g, ffn_w_gate, ffn_w_up, ffn_w_down, final_norm_g):
    bsz, s, d = x.shape
    depth = w_in.shape[0]
    t = bsz * s
    assert d == D_MODEL and s % TS == 0
    tiles_per_seq = s // TS
    row = lambda v: v.reshape(1, -1).astype(F32)
    per_head = lambda v: jnp.repeat(v.astype(F32), B_HEAD_DIM).reshape(1, B_DIM)

    def proj_args(l):
        w_r = jnp.concatenate(
            [w_in[l, :, :OFF_GLU], w_in[l, :, OFF_GLU + B_HEADS:],
             jnp.pad(w_in[l, :, OFF_GLU:OFF_GLU + B_HEADS], ((0, 0), (0, DT_PAD - B_HEADS)))],
            axis=1).astype(BF16)
        dtb = jnp.pad(ssm_dt_bias[l].astype(F32), (0, DT_PAD - B_HEADS)).reshape(1, DT_PAD)
        return dict(
            proj=(row(norm1_g[l]), w_r, dtb, per_head(ssm_a_log[l])),
            prep=(ssm_conv_w[l].astype(F32), row(ssm_conv_b[l]),
                  per_head(ssm_d[l]), conv_dw_w[l].astype(F32), row(conv_dw_b[l]),
                  row(conv_ln_g[l]), row(conv_ln_b[l])))

    def ffn_args(l, mix_ab, mix_c):
        return (mix_ab.reshape(t, D_AB), mix_c, w_out[l].astype(BF16), row(norm2_g[l]),
                ffn_w_gate[l].astype(BF16), ffn_w_up[l].astype(BF16), ffn_w_down[l].astype(BF16))

    x2 = x.reshape(t, d)
    acts = _dense_call(x2, tiles_per_seq, **proj_args(0))
    for l in range(depth):
        *seq_acts, mix_c = acts
        q, kv, zs, bc, xdt, xdte, scores, fstart, xsd = (a.reshape(bsz, s, -1) for a in seq_acts)
        mix_ab = _mixer_call(q, kv, zs, bc, xdt, xdte, scores, fstart, xsd,
                             attn_rel_bias[l].astype(F32), row(ssm_norm_g[l]))
        if l + 1 < depth:
            x2, *acts = _dense_call(x2, tiles_per_seq, ffn=ffn_args(l, mix_ab, mix_c),
                                    **proj_args(l + 1))
        else:
            (x2,) = _dense_call(x2, tiles_per_seq, ffn=ffn_args(l, mix_ab, mix_c),
                                final_g=row(final_norm_g))
    return x2.reshape(bsz, s, d)
```

```python
import functools

import jax
import jax.numpy as jnp
from jax import lax
from jax.experimental import pallas as pl
from jax.experimental.pallas import tpu as pltpu

F32 = jnp.float32
BF16 = jnp.bfloat16

D_MODEL = 1024
CHUNK = 64
A_HEADS = 4
A_DIM = 256
LEFT_CHUNKS = 8
LEFT = LEFT_CHUNKS * CHUNK
MAX_REL = 128
N_REL = 2 * MAX_REL + 1
B_HEADS = 8
B_HEAD_DIM = 64
B_DIM = 512
B_GROUPS = 2
B_STATE = 128
B_CONV = 4
XBC_DIM = 1024
C_DIM = 256
C_KERNEL = 31
D_MIX = 1024
D_AB = A_DIM + B_DIM
D_FF = 2816
EPS = 1e-6
NEG_INF = -1e30

OFF_QKV = 0
OFF_Z = 768
OFF_XBC = 1280
OFF_GLU = 2304
OFF_DT = 2816
N_PROJ = 2944
DT_PAD = 128

TS = 512
QB = 128
BAND = LEFT + QB
XBC_HALO = 8
GLU_HALO = 32
FF_SLICES = ((0, 512), (512, 1024), (1024, 1536), (1536, 2048), (2048, 2816))
VMEM_LIMIT = 56 * 1024 * 1024


def _dot(a, b):
    return jnp.dot(a, b, preferred_element_type=F32)


def _dot_t(a, b):
    return lax.dot_general(a, b, (((1,), (1,)), ((), ())), preferred_element_type=F32)


def _tdot(a, b):
    return lax.dot_general(a, b, (((0,), (0,)), ((), ())), preferred_element_type=F32)


def _split3(a):
    hi = a.astype(BF16)
    r1 = a - hi.astype(F32)
    mid = r1.astype(BF16)
    lo = (r1 - mid.astype(F32)).astype(BF16)
    return hi, mid, lo


def _sel_left(m01, a):
    hi, mid, lo = _split3(a)
    return (_dot(m01, lo) + _dot(m01, mid)) + _dot(m01, hi)


def _rms(x, g):
    return x * lax.rsqrt(jnp.mean(x * x, axis=-1, keepdims=True) + EPS) * g


def _silu(x):
    return x * jax.nn.sigmoid(x)


def _mul_rows8(x, w8):
    n = x.shape[0] // 8
    return (x.reshape(n, 8, x.shape[1]) * w8[None]).reshape(x.shape)


def _iota(shape, dim):
    return lax.broadcasted_iota(jnp.int32, shape, dim)


def _mixer_prep(first, xbc_s, glu_s, dte_s, acs_s, seg_s, convw_ref, convb_ref, dskip_ref,
                dww_ref, dwb_ref, lng_ref, lnb_ref,
                bc_ref, xdt_ref, xdte_ref, scores_ref, fstart_ref, xsd_ref, outc_ref):
    causal = _iota((CHUNK, B_DIM), 0) >= jnp.bitwise_and(_iota((CHUNK, B_DIM), 1), CHUNK - 1)
    all_ones = jnp.ones((C_DIM, C_DIM), BF16)

    def begin():
        xbc_s[0:XBC_HALO, :] = jnp.where(first, 0.0, xbc_s[0:XBC_HALO, :])
        glu_s[0:GLU_HALO, :] = jnp.where(first, 0.0, glu_s[0:GLU_HALO, :])

    def chunk(c):
        r0 = c * CHUNK
        rows = slice(r0, r0 + CHUNK)
        win = xbc_s[r0:r0 + XBC_HALO + CHUNK, :]
        acc = convb_ref[...] + _mul_rows8(win[XBC_HALO:, :], convw_ref[8 * (B_CONV - 1):8 * B_CONV, :])
        for k in range(B_CONV - 1):
            off = XBC_HALO - (B_CONV - 1) + k
            shifted = pltpu.roll(win, XBC_HALO + CHUNK - off, axis=0)[:CHUNK, :]
            acc = acc + _mul_rows8(shifted, convw_ref[8 * k:8 * k + 8, :])
        xc = _silu(acc)
        xs = xc[:, :B_DIM]
        bc = xc[:, B_DIM:].astype(BF16)
        bc_ref[rows, :] = bc
        decay = jnp.exp(jnp.where(causal, seg_s[rows, :], -jnp.inf))
        for g in range(B_GROUPS):
            bg = bc[:, g * B_STATE:(g + 1) * B_STATE]
            cg = bc[:, (B_GROUPS + g) * B_STATE:(B_GROUPS + g + 1) * B_STATE]
            cb = _dot_t(cg, jnp.concatenate([bg] * 4, axis=0))
            scores_ref[rows, g * 256:(g + 1) * 256] = (cb * decay[:, g * 256:(g + 1) * 256]).astype(BF16)
        acs = acs_s[rows, :]
        fstart_ref[rows, :] = jnp.exp(acs)
        xdt = xs * dte_s[rows, :]
        xdt_ref[rows, :] = xdt.astype(BF16)
        xdte_ref[rows, :] = (xdt * jnp.exp(acs[CHUNK - 1:CHUNK, :] - acs)).astype(BF16)
        xsd_ref[rows, :] = (xs * dskip_ref[...]).astype(BF16)
        acc = jnp.broadcast_to(dwb_ref[...], (CHUNK, C_DIM))
        for rho in range(8):
            nrow = CHUNK if rho == 0 else CHUNK + 8
            part = None
            for k in range(C_KERNEL):
                off = GLU_HALO - (C_KERNEL - 1) + k
                if off % 8 != rho:
                    continue
                start = r0 + off - rho
                term = _mul_rows8(glu_s[start:start + nrow, :], dww_ref[8 * k:8 * k + 8, :])
                part = term if part is None else part + term
            acc = acc + (part if rho == 0 else pltpu.roll(part, nrow - rho, axis=0)[:CHUNK, :])
        hi = acc.astype(BF16)
        mid = (acc - hi.astype(F32)).astype(BF16)
        mu = (_dot(mid, all_ones) + _dot(hi, all_ones)) * (1.0 / C_DIM)
        d = acc - mu
        var = _dot((d * d).astype(BF16), all_ones) * (1.0 / C_DIM)
        y = d * lax.rsqrt(var + EPS) * lng_ref[...] + lnb_ref[...]
        outc_ref[rows, :] = _silu(y).astype(BF16)

    def end():
        xbc_s[0:XBC_HALO, :] = xbc_s[TS:TS + XBC_HALO, :]
        glu_s[0:GLU_HALO, :] = glu_s[TS:TS + GLU_HALO, :]

    return begin, chunk, end


def _dense_kernel(*refs, has_ffn, has_proj, final, tiles_per_seq):
    it = iter(refs)
    take = lambda n: [next(it) for _ in range(n)]
    (x_ref,) = take(1)
    if has_ffn:
        mixab_ref, mixc_ref, wout_ref, n2g_ref, wg_ref, wu_ref, wd_ref = take(7)
    if final:
        (fg_ref,) = take(1)
    if has_proj:
        n1g_ref, win_ref, dtb_ref, alog_ref = take(4)
        prep_params = take(7)
    if has_ffn:
        (xo_ref,) = take(1)
    if has_proj:
        q_ref, kv_ref, zs_ref = take(3)
        prep_outs = take(7)
        scratch = take(5)
        xbc_s, glu_s, dte_s, acs_s, seg_s = scratch

    pending = []
    if has_proj:
        i = pl.program_id(0)

        @pl.when(i == 0)
        def _init():
            for ref in scratch:
                ref[...] = jnp.zeros(ref.shape, F32)

        first = lax.rem(i + (tiles_per_seq - 1), tiles_per_seq) == 0
        prep_begin, prep_chunk, prep_end = _mixer_prep(first, *scratch, *prep_params, *prep_outs)
        prep_begin()
        pending = list(range(TS // CHUNK))

    def prep_some(n):
        for _ in range(min(n, len(pending))):
            prep_chunk(pending.pop(0))

    x = x_ref[...]
    if has_ffn:
        mix = jnp.concatenate([mixab_ref[...], mixc_ref[...]], axis=1)
        xn = x + _dot(mix, wout_ref[...])
        h = _rms(xn, n2g_ref[...]).astype(BF16)
        x = xn
        for lo, hi in FF_SLICES:
            g = _dot(h, wg_ref[:, lo:hi])
            u = _dot(h, wu_ref[:, lo:hi])
            x = x + _dot((_silu(g) * u).astype(BF16), wd_ref[lo:hi, :])
            prep_some(2)
        xo_ref[...] = _rms(x, fg_ref[...]) if final else x
    if has_proj:
        h = _rms(x, n1g_ref[...]).astype(BF16)
        dt_raw = _dot(h, win_ref[:, OFF_DT:N_PROJ]) + dtb_ref[...]
        dt = jnp.maximum(dt_raw, 0.0) + jnp.log1p(jnp.exp(-jnp.abs(dt_raw)))
        prep_some(len(pending))
        prep_end()
        qkv = _dot(h, win_ref[:, OFF_QKV:OFF_Z])
        q_ref[...] = (qkv[:, :A_DIM] * 0.125).astype(BF16)
        kv_ref[...] = qkv[:, A_DIM:].astype(BF16)
        low_half = _iota((TS, 128), 1) < B_HEAD_DIM
        for pr in range(B_HEADS // 2):
            dte_s[:, pr * 128:(pr + 1) * 128] = jnp.where(
                low_half,
                jnp.broadcast_to(dt[:, 2 * pr:2 * pr + 1], (TS, 128)),
                jnp.broadcast_to(dt[:, 2 * pr + 1:2 * pr + 2], (TS, 128)))
        zs_ref[...] = _silu(_dot(h, win_ref[:, OFF_Z:OFF_XBC])).astype(BF16)
        a_neg = -jnp.exp(alog_ref[...])
        l_i = _iota((CHUNK, B_DIM), 0)
        s_i = jnp.bitwise_and(_iota((CHUNK, B_DIM), 1), CHUNK - 1)
        diag = (l_i == s_i).astype(F32)
        tri = (_iota((CHUNK, CHUNK), 0) >= _iota((CHUNK, CHUNK), 1)).astype(BF16)
        ones = jnp.ones((CHUNK, CHUNK), BF16)
        chunk_rows = [slice(c * CHUNK, (c + 1) * CHUNK) for c in range(TS // CHUNK)]
        for rows in chunk_rows:
            acs_s[rows, :] = _sel_left(tri, dte_s[rows, :] * a_neg)
        xbc_s[XBC_HALO:XBC_HALO + TS, :] = _dot(h, win_ref[:, OFF_XBC:OFF_GLU])
        for rows in chunk_rows:
            acs_c = acs_s[rows, :]
            seg_s[rows, :] = acs_c - _sel_left(ones, acs_c * diag)
        u = _dot(h, win_ref[:, OFF_GLU:OFF_DT])
        glu_s[GLU_HALO:GLU_HALO + TS, :] = u[:, :C_DIM] * jax.nn.sigmoid(u[:, C_DIM:])


def _dense_call(x2, tiles_per_seq, ffn=None, final_g=None, proj=None, prep=None):
    t = x2.shape[0]
    n_tiles = t // TS
    has_ffn, has_proj, final = ffn is not None, proj is not None, final_g is not None
    last = n_tiles - 1

    def now(c):
        return pl.BlockSpec((TS, c), lambda i: (jnp.minimum(i, last), 0))

    def before(c):
        return pl.BlockSpec((TS, c), lambda i: (jnp.maximum(i - 1, 0), 0))

    def resident(a):
        return pl.BlockSpec(a.shape, lambda i: (0, 0), pipeline_mode=pl.Buffered(1))

    args, in_specs, out_shape, out_specs, scratch = [x2], [now(D_MODEL)], [], [], []
    if has_ffn:
        args += list(ffn)
        in_specs += [now(D_AB), now(C_DIM)] + [resident(a) for a in ffn[2:]]
        out_shape.append(jax.ShapeDtypeStruct((t, D_MODEL), F32))
        out_specs.append(now(D_MODEL))
    if final:
        args.append(final_g)
        in_specs.append(resident(final_g))
    if has_proj:
        args += list(proj) + list(prep)
        in_specs += [resident(a) for a in list(proj) + list(prep)]
        for c, dt in ((A_DIM, BF16), (2 * A_DIM, BF16), (B_DIM, BF16)):
            out_shape.append(jax.ShapeDtypeStruct((t, c), dt))
            out_specs.append(now(c))
        for c, dt in ((B_DIM, BF16), (B_DIM, BF16), (B_DIM, BF16), (B_DIM, BF16), (B_DIM, F32),
                      (B_DIM, BF16), (C_DIM, BF16)):
            out_shape.append(jax.ShapeDtypeStruct((t, c), dt))
            out_specs.append(before(c))
        scratch = [pltpu.VMEM((XBC_HALO + TS, XBC_DIM), F32),
                   pltpu.VMEM((GLU_HALO + TS, C_DIM), F32),
                   pltpu.VMEM((TS, B_DIM), F32),
                   pltpu.VMEM((TS, B_DIM), F32),
                   pltpu.VMEM((TS, B_DIM), F32)]
    return pl.pallas_call(
        functools.partial(_dense_kernel, has_ffn=has_ffn, has_proj=has_proj, final=final,
                          tiles_per_seq=tiles_per_seq),
        grid=(n_tiles + (1 if has_proj else 0),),
        in_specs=in_specs,
        out_specs=out_specs,
        out_shape=out_shape,
        scratch_shapes=scratch,
        compiler_params=pltpu.CompilerParams(
            dimension_semantics=("arbitrary",),
            vmem_limit_bytes=VMEM_LIMIT,
            ),
        name="dense_" + ("f" if has_ffn else "") + ("p" if has_proj else "") + ("n" if final else ""),
    )(*args)


def _mixer_kernel(q_ref, kv_ref, kvp_ref, zs_ref, bc_ref, xdt_ref, xdte_ref, scores_ref, fstart_ref,
                  xsd_ref, relb_ref, normg_ref,
                  o_ref,
                  bias_s, st_s):
    b = pl.program_id(0)
    i = pl.program_id(1)

    @pl.when(jnp.logical_and(b == 0, i == 0))
    def _build_bias():
        gvw = BAND + QB
        d = _iota((8, gvw), 1) - _iota((8, gvw), 0) - (QB - 1)
        idx = jnp.clip(LEFT - d, -MAX_REL, MAX_REL) + MAX_REL
        rr = _iota((2 * QB, BAND), 0)
        c = _iota((2 * QB, BAND), 1)
        qc = jnp.right_shift(jnp.bitwise_and(rr, QB - 1), 6)
        kc = jnp.right_shift(c, 6)
        valid = jnp.logical_and(kc >= qc, kc <= qc + LEFT_CHUNKS)
        for pr in range(2):
            halves = []
            for head in (2 * pr, 2 * pr + 1):
                gv8 = lax.fori_loop(
                    0, N_REL, lambda v, acc, head=head: jnp.where(idx == v, relb_ref[head, v], acc),
                    jnp.zeros((8, gvw), F32))
                halves += [gv8[:, QB - 1 - 8 * blk:QB - 1 - 8 * blk + BAND] for blk in range(QB // 8)]
            bias_s[pr] = jnp.where(valid, jnp.concatenate(halves, axis=0), NEG_INF)

    @pl.when(i == 0)
    def _reset():
        st_s[...] = jnp.zeros((B_GROUPS, B_STATE, 256), F32)

    lane = _iota((QB, 128), 1)
    first_head = lane < 64
    def attn_scores(j, pr):
        r0 = j * QB
        qa = q_ref[0, r0:r0 + QB, pr * 128:(pr + 1) * 128]
        qq = jnp.concatenate([jnp.where(first_head, qa, jnp.zeros_like(qa)),
                              jnp.where(first_head, jnp.zeros_like(qa), qa)], axis=0)
        kcol = slice(pr * 128, (pr + 1) * 128)
        kb = jnp.concatenate([kvp_ref[0, r0:TS, kcol], kv_ref[0, 0:r0 + QB, kcol]], axis=0)
        return _dot_t(qq, kb)

    def attn_finish(j, pr, s):
        r0 = j * QB
        col_ok = (_iota((1, BAND), 1) + (i * TS + r0 - LEFT)) >= 0
        s = jnp.where(col_ok, s + bias_s[pr], NEG_INF)
        m = jnp.max(s, axis=-1, keepdims=True)
        p = jnp.exp(s - m)
        l = jnp.sum(p, axis=-1, keepdims=True)
        vcol = slice(A_DIM + pr * 128, A_DIM + (pr + 1) * 128)
        vb = jnp.concatenate([kvp_ref[0, r0:TS, vcol], kv_ref[0, 0:r0 + QB, vcol]], axis=0)
        o = _dot(p.astype(BF16), vb) / l
        o_ref[0, r0:r0 + QB, pr * 128:(pr + 1) * 128] = jnp.where(
            first_head, o[:QB], o[QB:]).astype(BF16)

    head_of_lane = jnp.right_shift(_iota((1, 256), 1), 6)
    head_mask = [(head_of_lane == hh).astype(BF16) for hh in range(4)]

    def ssd_stateless(c):
        rows = slice(c * CHUNK, (c + 1) * CHUNK)
        out = []
        for g in range(B_GROUPS):
            cols = slice(g * 256, (g + 1) * 256)
            bg = bc_ref[0, rows, g * B_STATE:(g + 1) * B_STATE]
            xg = xdt_ref[0, rows, cols]
            x_bd = jnp.concatenate([xg * head_mask[hh] for hh in range(4)], axis=0)
            out.append((_dot(scores_ref[0, rows, cols], x_bd), _tdot(bg, xdte_ref[0, rows, cols])))
        return out

    def ssd_scan(c, stateless):
        r0 = c * CHUNK
        rows = slice(r0, r0 + CHUNK)
        for g in range(B_GROUPS):
            cols = slice(g * 256, (g + 1) * 256)
            y_intra, state_in = stateless[g]
            cg = bc_ref[0, rows, (B_GROUPS + g) * B_STATE:(B_GROUPS + g + 1) * B_STATE]
            state = st_s[g]
            y = y_intra + _dot(cg, state.astype(BF16)) * fstart_ref[0, rows, cols]
            chunk_decay = fstart_ref[0, r0 + CHUNK - 1:r0 + CHUNK, cols]
            st_s[g] = state * chunk_decay + state_in
            y = (y + xsd_ref[0, rows, cols]) * zs_ref[0, rows, cols]
            y = y * lax.rsqrt(jnp.mean(y * y, axis=-1, keepdims=True) + EPS) * normg_ref[:, cols]
            o_ref[0, rows, A_DIM + g * 256:A_DIM + (g + 1) * 256] = y.astype(BF16)

    units = [(j, pr) for j in range(TS // QB) for pr in range(2)]
    n_chunks = TS // CHUNK
    assert len(units) == n_chunks
    s_next = attn_scores(*units[0])
    stateless_next = ssd_stateless(0)
    for k in range(n_chunks):
        s_cur, stateless_cur = s_next, stateless_next
        if k + 1 < n_chunks:
            s_next = attn_scores(*units[k + 1])
        attn_finish(*units[k], s_cur)
        if k + 1 < n_chunks:
            stateless_next = ssd_stateless(k + 1)
        ssd_scan(k, stateless_cur)


def _mixer_call(q, kv, zs, bc, xdt, xdte, scores, fstart, xsd, relb, normg):
    bsz, s, _ = q.shape

    def tile(c):
        return pl.BlockSpec((1, TS, c), lambda b, i: (b, i, 0))

    in_specs = [
        tile(A_DIM), tile(2 * A_DIM),
        pl.BlockSpec((1, TS, 2 * A_DIM), lambda b, i: (b, jnp.maximum(i - 1, 0), 0)),
        tile(B_DIM), tile(B_DIM), tile(B_DIM), tile(B_DIM), tile(B_DIM), tile(B_DIM), tile(B_DIM),
        pl.BlockSpec(memory_space=pltpu.SMEM),
        pl.BlockSpec(normg.shape, lambda b, i: (0, 0)),
    ]
    scratch = [
        pltpu.VMEM((2, 2 * QB, BAND), F32),
        pltpu.VMEM((B_GROUPS, B_STATE, 256), F32),
    ]
    return pl.pallas_call(
        _mixer_kernel,
        grid=(bsz, s // TS),
        in_specs=in_specs,
        out_specs=pl.BlockSpec((1, TS, D_AB), lambda b, i: (b, i, 0)),
        out_shape=jax.ShapeDtypeStruct((bsz, s, D_AB), BF16),
        scratch_shapes=scratch,
        compiler_params=pltpu.CompilerParams(
            dimension_semantics=("arbitrary", "arbitrary"),
            vmem_limit_bytes=VMEM_LIMIT),
        name="mixer",
    )(q, kv, kv, zs, bc, xdt, xdte, scores, fstart, xsd, relb, normg)


def kernel(x, norm1_g, w_in, attn_rel_bias, ssm_conv_w, ssm_conv_b, ssm_dt_bias, ssm_a_log, ssm_d, ssm_norm_g, conv_dw_w, conv_dw_b, conv_ln_g, conv_ln_b, w_out, norm2_g, ffn_w_gate, ffn_w_up, ffn_w_down, final_norm_g):
    bsz, s, d = x.shape
    depth = w_in.shape[0]
    t = bsz * s
    assert d == D_MODEL and s % TS == 0
    tiles_per_seq = s // TS
    row = lambda v: v.reshape(1, -1).astype(F32)
    per_head = lambda v: jnp.repeat(v.astype(F32), B_HEAD_DIM).reshape(1, B_DIM)
    rows8 = lambda w: jnp.repeat(w.astype(F32), 8, axis=0)

    def proj_args(l):
        w_r = jnp.concatenate(
            [w_in[l, :, :OFF_GLU], w_in[l, :, OFF_GLU + B_HEADS:],
             jnp.pad(w_in[l, :, OFF_GLU:OFF_GLU + B_HEADS], ((0, 0), (0, DT_PAD - B_HEADS)))],
            axis=1).astype(BF16)
        dtb = jnp.pad(ssm_dt_bias[l].astype(F32), (0, DT_PAD - B_HEADS)).reshape(1, DT_PAD)
        return dict(
            proj=(row(norm1_g[l]), w_r, dtb, per_head(ssm_a_log[l])),
            prep=(rows8(ssm_conv_w[l]), row(ssm_conv_b[l]),
                  per_head(ssm_d[l]), rows8(conv_dw_w[l]),
                  row(conv_dw_b[l]),
                  row(conv_ln_g[l]), row(conv_ln_b[l])))

    def ffn_args(l, mix_ab, mix_c):
        return (mix_ab.reshape(t, D_AB), mix_c, w_out[l].astype(BF16), row(norm2_g[l]),
                ffn_w_gate[l].astype(BF16), ffn_w_up[l].astype(BF16), ffn_w_down[l].astype(BF16))

    x2 = x.reshape(t, d)
    acts = _dense_call(x2, tiles_per_seq, **proj_args(0))
    for l in range(depth):
        *seq_acts, mix_c = acts
        q, kv, zs, bc, xdt, xdte, scores, fstart, xsd = (a.reshape(bsz, s, -1) for a in seq_acts)
        mix_ab = _mixer_call(q, kv, zs, bc, xdt, xdte, scores, fstart, xsd,
                             attn_rel_bias[l].astype(F32), row(ssm_norm_g[l]))
        if l + 1 < depth:
            x2, *acts = _dense_call(x2, tiles_per_seq, ffn=ffn_args(l, mix_ab, mix_c),
                                    **proj_args(l + 1))
        else:
            (x2,) = _dense_call(x2, tiles_per_seq, ffn=ffn_args(l, mix_ab, mix_c),
                                final_g=row(final_norm_g))
    return x2.reshape(bsz, s, d)
```

```python
import functools

import jax
import jax.numpy as jnp
from jax import lax
from jax.experimental import pallas as pl
from jax.experimental.pallas import tpu as pltpu

F32 = jnp.float32
BF16 = jnp.bfloat16

D_MODEL = 1024
CHUNK = 64
A_HEADS = 4
A_DIM = 256
LEFT_CHUNKS = 8
LEFT = LEFT_CHUNKS * CHUNK
MAX_REL = 128
N_REL = 2 * MAX_REL + 1
B_HEADS = 8
B_HEAD_DIM = 64
B_DIM = 512
B_GROUPS = 2
B_STATE = 128
B_CONV = 4
XBC_DIM = 1024
C_DIM = 256
C_KERNEL = 31
D_MIX = 1024
D_AB = A_DIM + B_DIM
D_FF = 2816
EPS = 1e-6
NEG_INF = -1e30

OFF_QKV = 0
OFF_Z = 768
OFF_XBC = 1280
OFF_GLU = 2304
OFF_DT = 2816
N_PROJ = 2944
DT_PAD = 128

TS = 512
QB = 128
BAND = LEFT + QB
XBC_HALO = 8
GLU_HALO = 32
FF_SLICES = ((0, 512), (512, 1024), (1024, 1536), (1536, 2048), (2048, 2816))
VMEM_LIMIT = 56 * 1024 * 1024


def _dot(a, b):
    return jnp.dot(a, b, preferred_element_type=F32)


def _dot_t(a, b):
    return lax.dot_general(a, b, (((1,), (1,)), ((), ())), preferred_element_type=F32)


def _tdot(a, b):
    return lax.dot_general(a, b, (((0,), (0,)), ((), ())), preferred_element_type=F32)


def _split3(a):
    hi = a.astype(BF16)
    r1 = a - hi.astype(F32)
    mid = r1.astype(BF16)
    lo = (r1 - mid.astype(F32)).astype(BF16)
    return hi, mid, lo


def _sel_left(m01, a):
    hi, mid, lo = _split3(a)
    return (_dot(m01, lo) + _dot(m01, mid)) + _dot(m01, hi)


def _rms(x, g):
    return x * lax.rsqrt(jnp.mean(x * x, axis=-1, keepdims=True) + EPS) * g


def _silu(x):
    return x * jax.nn.sigmoid(x)


def _mul_rows8(x, w8):
    n = x.shape[0] // 8
    return (x.reshape(n, 8, x.shape[1]) * w8[None]).reshape(x.shape)


def _iota(shape, dim):
    return lax.broadcasted_iota(jnp.int32, shape, dim)


def _mixer_prep(first, xbc_s, glu_s, dte_s, acs_s, seg_s, convw_ref, convb_ref, dskip_ref,
                dww_ref, dwb_ref, lng_ref, lnb_ref,
                bc_ref, xdt_ref, xdte_ref, scores_ref, fstart_ref, xsd_ref, outc_ref):
    causal = _iota((CHUNK, B_DIM), 0) >= jnp.bitwise_and(_iota((CHUNK, B_DIM), 1), CHUNK - 1)
    all_ones = jnp.ones((C_DIM, C_DIM), BF16)

    def begin():
        xbc_s[0:XBC_HALO, :] = jnp.where(first, 0.0, xbc_s[0:XBC_HALO, :])
        glu_s[0:GLU_HALO, :] = jnp.where(first, 0.0, glu_s[0:GLU_HALO, :])

    def chunk(c):
        r0 = c * CHUNK
        rows = slice(r0, r0 + CHUNK)
        win = xbc_s[r0:r0 + XBC_HALO + CHUNK, :]
        acc = convb_ref[...] + _mul_rows8(win[XBC_HALO:, :], convw_ref[8 * (B_CONV - 1):8 * B_CONV, :])
        for k in range(B_CONV - 1):
            off = XBC_HALO - (B_CONV - 1) + k
            shifted = pltpu.roll(win, XBC_HALO + CHUNK - off, axis=0)[:CHUNK, :]
            acc = acc + _mul_rows8(shifted, convw_ref[8 * k:8 * k + 8, :])
        xc = _silu(acc)
        xs = xc[:, :B_DIM]
        bc = xc[:, B_DIM:].astype(BF16)
        bc_ref[rows, :] = bc
        decay = jnp.exp(jnp.where(causal, seg_s[rows, :], -jnp.inf))
        for g in range(B_GROUPS):
            bg = bc[:, g * B_STATE:(g + 1) * B_STATE]
            cg = bc[:, (B_GROUPS + g) * B_STATE:(B_GROUPS + g + 1) * B_STATE]
            cb = _dot_t(cg, jnp.concatenate([bg] * 4, axis=0))
            scores_ref[rows, g * 256:(g + 1) * 256] = (cb * decay[:, g * 256:(g + 1) * 256]).astype(BF16)
        acs = acs_s[rows, :]
        fstart_ref[rows, :] = jnp.exp(acs)
        xdt = xs * dte_s[rows, :]
        xdt_ref[rows, :] = xdt.astype(BF16)
        xdte_ref[rows, :] = (xdt * jnp.exp(acs[CHUNK - 1:CHUNK, :] - acs)).astype(BF16)
        xsd_ref[rows, :] = (xs * dskip_ref[...]).astype(BF16)
        acc = jnp.broadcast_to(dwb_ref[...], (CHUNK, C_DIM))
        for rho in range(8):
            nrow = CHUNK if rho == 0 else CHUNK + 8
            part = None
            for k in range(C_KERNEL):
                off = GLU_HALO - (C_KERNEL - 1) + k
                if off % 8 != rho:
                    continue
                start = r0 + off - rho
                term = _mul_rows8(glu_s[start:start + nrow, :], dww_ref[8 * k:8 * k + 8, :])
                part = term if part is None else part + term
            acc = acc + (part if rho == 0 else pltpu.roll(part, nrow - rho, axis=0)[:CHUNK, :])
        hi = acc.astype(BF16)
        mid = (acc - hi.astype(F32)).astype(BF16)
        mu = (_dot(mid, all_ones) + _dot(hi, all_ones)) * (1.0 / C_DIM)
        d = acc - mu
        var = _dot((d * d).astype(BF16), all_ones) * (1.0 / C_DIM)
        y = d * lax.rsqrt(var + EPS) * lng_ref[...] + lnb_ref[...]
        outc_ref[rows, :] = _silu(y).astype(BF16)

    def end():
        xbc_s[0:XBC_HALO, :] = xbc_s[TS:TS + XBC_HALO, :]
        glu_s[0:GLU_HALO, :] = glu_s[TS:TS + GLU_HALO, :]

    return begin, chunk, end


def _dense_kernel(*refs, has_ffn, has_proj, final, tiles_per_seq):
    it = iter(refs)
    take = lambda n: [next(it) for _ in range(n)]
    (x_ref,) = take(1)
    if has_ffn:
        mixab_ref, mixc_ref, wout_ref, n2g_ref, wg_ref, wu_ref, wd_ref = take(7)
    if final:
        (fg_ref,) = take(1)
    if has_proj:
        n1g_ref, win_ref, dtb_ref, alog_ref = take(4)
        prep_params = take(7)
    if has_ffn:
        (xo_ref,) = take(1)
    if has_proj:
        q_ref, kv_ref, zs_ref = take(3)
        prep_outs = take(7)
        scratch = take(5)
        xbc_s, glu_s, dte_s, acs_s, seg_s = scratch

    if has_proj:
        i = pl.program_id(0)
        n_tiles = pl.num_programs(0) - 1

        @pl.when(i == 0)
        def _init():
            for ref in scratch:
                ref[...] = jnp.zeros(ref.shape, F32)

        first = lax.rem(i + (tiles_per_seq - 1), tiles_per_seq) == 0

    def step(project):
        pending = []
        if has_proj:
            prep_begin, prep_chunk, prep_end = _mixer_prep(first, *scratch, *prep_params, *prep_outs)
            prep_begin()
            pending = list(range(TS // CHUNK))

        def prep_some(n):
            for _ in range(min(n, len(pending))):
                prep_chunk(pending.pop(0))

        if not project:
            prep_some(len(pending))
            return
        x = x_ref[...]
        if has_ffn:
            mix = jnp.concatenate([mixab_ref[...], mixc_ref[...]], axis=1)
            xn = x + _dot(mix, wout_ref[...])
            h = _rms(xn, n2g_ref[...]).astype(BF16)
            x = xn
            for lo, hi in FF_SLICES:
                g = _dot(h, wg_ref[:, lo:hi])
                u = _dot(h, wu_ref[:, lo:hi])
                x = x + _dot((_silu(g) * u).astype(BF16), wd_ref[lo:hi, :])
                prep_some(2)
            xo_ref[...] = _rms(x, fg_ref[...]) if final else x
        if has_proj:
            h = _rms(x, n1g_ref[...]).astype(BF16)
            dt_raw = _dot(h, win_ref[:, OFF_DT:N_PROJ]) + dtb_ref[...]
            dt = jnp.maximum(dt_raw, 0.0) + jnp.log1p(jnp.exp(-jnp.abs(dt_raw)))
            prep_some(len(pending))
            prep_end()
            qkv = _dot(h, win_ref[:, OFF_QKV:OFF_Z])
            q_ref[...] = (qkv[:, :A_DIM] * 0.125).astype(BF16)
            kv_ref[...] = qkv[:, A_DIM:].astype(BF16)
            low_half = _iota((TS, 128), 1) < B_HEAD_DIM
            for pr in range(B_HEADS // 2):
                dte_s[:, pr * 128:(pr + 1) * 128] = jnp.where(
                    low_half,
                    jnp.broadcast_to(dt[:, 2 * pr:2 * pr + 1], (TS, 128)),
                    jnp.broadcast_to(dt[:, 2 * pr + 1:2 * pr + 2], (TS, 128)))
            zs_ref[...] = _silu(_dot(h, win_ref[:, OFF_Z:OFF_XBC])).astype(BF16)
            a_neg = -jnp.exp(alog_ref[...])
            l_i = _iota((CHUNK, B_DIM), 0)
            s_i = jnp.bitwise_and(_iota((CHUNK, B_DIM), 1), CHUNK - 1)
            diag = (l_i == s_i).astype(F32)
            tri = (_iota((CHUNK, CHUNK), 0) >= _iota((CHUNK, CHUNK), 1)).astype(BF16)
            ones = jnp.ones((CHUNK, CHUNK), BF16)
            chunk_rows = [slice(c * CHUNK, (c + 1) * CHUNK) for c in range(TS // CHUNK)]
            for rows in chunk_rows:
                acs_s[rows, :] = _sel_left(tri, dte_s[rows, :] * a_neg)
            xbc_s[XBC_HALO:XBC_HALO + TS, :] = _dot(h, win_ref[:, OFF_XBC:OFF_GLU])
            for rows in chunk_rows:
                acs_c = acs_s[rows, :]
                seg_s[rows, :] = acs_c - _sel_left(ones, acs_c * diag)
            u = _dot(h, win_ref[:, OFF_GLU:OFF_DT])
            glu_s[GLU_HALO:GLU_HALO + TS, :] = u[:, :C_DIM] * jax.nn.sigmoid(u[:, C_DIM:])

    if has_proj:
        pl.when(i < n_tiles)(lambda: step(True))
        pl.when(i == n_tiles)(lambda: step(False))
    else:
        step(True)


def _dense_call(x2, tiles_per_seq, ffn=None, final_g=None, proj=None, prep=None):
    t = x2.shape[0]
    n_tiles = t // TS
    has_ffn, has_proj, final = ffn is not None, proj is not None, final_g is not None
    last = n_tiles - 1

    def now(c):
        return pl.BlockSpec((TS, c), lambda i: (jnp.minimum(i, last), 0))

    def before(c):
        return pl.BlockSpec((TS, c), lambda i: (jnp.maximum(i - 1, 0), 0))

    def resident(a):
        if isinstance(a, tuple):
            w, layer = a
            return pl.BlockSpec((None,) + w.shape[1:], lambda i: (layer, 0, 0),
                                pipeline_mode=pl.Buffered(1))
        return pl.BlockSpec(a.shape, lambda i: (0, 0), pipeline_mode=pl.Buffered(1))

    unwrap = lambda a: a[0] if isinstance(a, tuple) else a
    args, in_specs, out_shape, out_specs, scratch = [x2], [now(D_MODEL)], [], [], []
    if has_ffn:
        args += [unwrap(a) for a in ffn]
        in_specs += [now(D_AB), now(C_DIM)] + [resident(a) for a in ffn[2:]]
        out_shape.append(jax.ShapeDtypeStruct((t, D_MODEL), F32))
        out_specs.append(now(D_MODEL))
    if final:
        args.append(final_g)
        in_specs.append(resident(final_g))
    if has_proj:
        args += [unwrap(a) for a in list(proj) + list(prep)]
        in_specs += [resident(a) for a in list(proj) + list(prep)]
        for c, dt in ((A_DIM, BF16), (2 * A_DIM, BF16), (B_DIM, BF16)):
            out_shape.append(jax.ShapeDtypeStruct((t, c), dt))
            out_specs.append(now(c))
        for c, dt in ((B_DIM, BF16), (B_DIM, BF16), (B_DIM, BF16), (B_DIM, BF16), (B_DIM, F32),
                      (B_DIM, BF16), (C_DIM, BF16)):
            out_shape.append(jax.ShapeDtypeStruct((t, c), dt))
            out_specs.append(before(c))
        scratch = [pltpu.VMEM((XBC_HALO + TS, XBC_DIM), F32),
                   pltpu.VMEM((GLU_HALO + TS, C_DIM), F32),
                   pltpu.VMEM((TS, B_DIM), F32),
                   pltpu.VMEM((TS, B_DIM), F32),
                   pltpu.VMEM((TS, B_DIM), F32)]
    return pl.pallas_call(
        functools.partial(_dense_kernel, has_ffn=has_ffn, has_proj=has_proj, final=final,
                          tiles_per_seq=tiles_per_seq),
        grid=(n_tiles + (1 if has_proj else 0),),
        in_specs=in_specs,
        out_specs=out_specs,
        out_shape=out_shape,
        scratch_shapes=scratch,
        compiler_params=pltpu.CompilerParams(
            dimension_semantics=("arbitrary",),
            vmem_limit_bytes=VMEM_LIMIT,
            ),
        name="dense_" + ("f" if has_ffn else "") + ("p" if has_proj else "") + ("n" if final else ""),
    )(*args)


def _mixer_kernel(q_ref, kv_ref, kvp_ref, zs_ref, bc_ref, xdt_ref, xdte_ref, scores_ref, fstart_ref,
                  xsd_ref, relb_ref, normg_ref,
                  o_ref,
                  bias_s, st_s):
    b = pl.program_id(0)
    i = pl.program_id(1)

    @pl.when(jnp.logical_and(b == 0, i == 0))
    def _build_bias():
        gvw = BAND + QB
        d = _iota((8, gvw), 1) - _iota((8, gvw), 0) - (QB - 1)
        idx = jnp.clip(LEFT - d, -MAX_REL, MAX_REL) + MAX_REL
        rr = _iota((2 * QB, BAND), 0)
        c = _iota((2 * QB, BAND), 1)
        qc = jnp.right_shift(jnp.bitwise_and(rr, QB - 1), 6)
        kc = jnp.right_shift(c, 6)
        valid = jnp.logical_and(kc >= qc, kc <= qc + LEFT_CHUNKS)
        for pr in range(2):
            halves = []
            for head in (2 * pr, 2 * pr + 1):
                gv8 = lax.fori_loop(
                    0, N_REL, lambda v, acc, head=head: jnp.where(idx == v, relb_ref[head, v], acc),
                    jnp.zeros((8, gvw), F32))
                halves += [gv8[:, QB - 1 - 8 * blk:QB - 1 - 8 * blk + BAND] for blk in range(QB // 8)]
            bias_s[pr] = jnp.where(valid, jnp.concatenate(halves, axis=0), NEG_INF)

    @pl.when(i == 0)
    def _reset():
        st_s[...] = jnp.zeros((B_GROUPS, B_STATE, 256), F32)

    lane = _iota((QB, 128), 1)
    first_head = lane < 64
    def attn_scores(j, pr):
        r0 = j * QB
        qa = q_ref[0, r0:r0 + QB, pr * 128:(pr + 1) * 128]
        qq = jnp.concatenate([jnp.where(first_head, qa, jnp.zeros_like(qa)),
                              jnp.where(first_head, jnp.zeros_like(qa), qa)], axis=0)
        kcol = slice(pr * 128, (pr + 1) * 128)
        kb = jnp.concatenate([kvp_ref[0, r0:TS, kcol], kv_ref[0, 0:r0 + QB, kcol]], axis=0)
        return _dot_t(qq, kb)

    def attn_finish(j, pr, s):
        r0 = j * QB
        col_ok = (_iota((1, BAND), 1) + (i * TS + r0 - LEFT)) >= 0
        s = jnp.where(col_ok, s + bias_s[pr], NEG_INF)
        m = jnp.max(s, axis=-1, keepdims=True)
        p = jnp.exp(s - m)
        l = jnp.sum(p, axis=-1, keepdims=True)
        vcol = slice(A_DIM + pr * 128, A_DIM + (pr + 1) * 128)
        vb = jnp.concatenate([kvp_ref[0, r0:TS, vcol], kv_ref[0, 0:r0 + QB, vcol]], axis=0)
        o = _dot(p.astype(BF16), vb) / l
        o_ref[0, r0:r0 + QB, pr * 128:(pr + 1) * 128] = jnp.where(
            first_head, o[:QB], o[QB:]).astype(BF16)

    head_of_lane = jnp.right_shift(_iota((1, 256), 1), 6)
    head_mask = [(head_of_lane == hh).astype(BF16) for hh in range(4)]

    def ssd_stateless(c):
        rows = slice(c * CHUNK, (c + 1) * CHUNK)
        out = []
        for g in range(B_GROUPS):
            cols = slice(g * 256, (g + 1) * 256)
            bg = bc_ref[0, rows, g * B_STATE:(g + 1) * B_STATE]
            xg = xdt_ref[0, rows, cols]
            x_bd = jnp.concatenate([xg * head_mask[hh] for hh in range(4)], axis=0)
            out.append((_dot(scores_ref[0, rows, cols], x_bd), _tdot(bg, xdte_ref[0, rows, cols])))
        return out

    def ssd_scan(c, stateless):
        r0 = c * CHUNK
        rows = slice(r0, r0 + CHUNK)
        for g in range(B_GROUPS):
            cols = slice(g * 256, (g + 1) * 256)
            y_intra, state_in = stateless[g]
            cg = bc_ref[0, rows, (B_GROUPS + g) * B_STATE:(B_GROUPS + g + 1) * B_STATE]
            state = st_s[g]
            y = y_intra + _dot(cg, state.astype(BF16)) * fstart_ref[0, rows, cols]
            chunk_decay = fstart_ref[0, r0 + CHUNK - 1:r0 + CHUNK, cols]
            st_s[g] = state * chunk_decay + state_in
            y = (y + xsd_ref[0, rows, cols]) * zs_ref[0, rows, cols]
            y = y * lax.rsqrt(jnp.mean(y * y, axis=-1, keepdims=True) + EPS) * normg_ref[:, cols]
            o_ref[0, rows, A_DIM + g * 256:A_DIM + (g + 1) * 256] = y.astype(BF16)

    units = [(j, pr) for j in range(TS // QB) for pr in range(2)]
    n_chunks = TS // CHUNK
    assert len(units) == n_chunks
    s_next = attn_scores(*units[0])
    stateless_next = ssd_stateless(0)
    for k in range(n_chunks):
        s_cur, stateless_cur = s_next, stateless_next
        if k + 1 < n_chunks:
            s_next = attn_scores(*units[k + 1])
        attn_finish(*units[k], s_cur)
        if k + 1 < n_chunks:
            stateless_next = ssd_stateless(k + 1)
        ssd_scan(k, stateless_cur)


def _mixer_call(q, kv, zs, bc, xdt, xdte, scores, fstart, xsd, relb, normg):
    bsz, s, _ = q.shape

    def tile(c):
        return pl.BlockSpec((1, TS, c), lambda b, i: (b, i, 0))

    in_specs = [
        tile(A_DIM), tile(2 * A_DIM),
        pl.BlockSpec((1, TS, 2 * A_DIM), lambda b, i: (b, jnp.maximum(i - 1, 0), 0)),
        tile(B_DIM), tile(B_DIM), tile(B_DIM), tile(B_DIM), tile(B_DIM), tile(B_DIM), tile(B_DIM),
        pl.BlockSpec(memory_space=pltpu.SMEM),
        pl.BlockSpec(normg.shape, lambda b, i: (0, 0)),
    ]
    scratch = [
        pltpu.VMEM((2, 2 * QB, BAND), F32),
        pltpu.VMEM((B_GROUPS, B_STATE, 256), F32),
    ]
    return pl.pallas_call(
        _mixer_kernel,
        grid=(bsz, s // TS),
        in_specs=in_specs,
        out_specs=pl.BlockSpec((1, TS, D_AB), lambda b, i: (b, i, 0)),
        out_shape=jax.ShapeDtypeStruct((bsz, s, D_AB), BF16),
        scratch_shapes=scratch,
        compiler_params=pltpu.CompilerParams(
            dimension_semantics=("arbitrary", "arbitrary"),
            vmem_limit_bytes=VMEM_LIMIT),
        name="mixer",
    )(q, kv, kv, zs, bc, xdt, xdte, scores, fstart, xsd, relb, normg)


def _cast_kernel(*refs, n, relayout):
    for k, (i_ref, o_ref) in enumerate(zip(refs[:n], refs[n:])):
        if k == relayout:
            w = i_ref[...]
            o_ref[:, 0:OFF_GLU] = w[:, 0:OFF_GLU].astype(BF16)
            o_ref[:, OFF_GLU:OFF_DT] = w[:, OFF_GLU + B_HEADS:].astype(BF16)
            o_ref[:, OFF_DT:N_PROJ] = jnp.concatenate(
                [w[:, OFF_GLU:OFF_GLU + B_HEADS], jnp.zeros((w.shape[0], DT_PAD - B_HEADS), F32)],
                axis=1).astype(BF16)
        else:
            o_ref[...] = i_ref[...].astype(BF16)


def _cast_call(arrays, n_steps, relayout=None):
    rows = arrays[0].shape[0]
    br = rows // n_steps
    out_cols = [N_PROJ if k == relayout else a.shape[1] for k, a in enumerate(arrays)]
    return pl.pallas_call(
        functools.partial(_cast_kernel, n=len(arrays), relayout=relayout),
        grid=(n_steps,),
        in_specs=[pl.BlockSpec((br, a.shape[1]), lambda i: (i, 0)) for a in arrays],
        out_specs=[pl.BlockSpec((br, c), lambda i: (i, 0)) for c in out_cols],
        out_shape=[jax.ShapeDtypeStruct((rows, c), BF16) for c in out_cols],
        compiler_params=pltpu.CompilerParams(
            dimension_semantics=("arbitrary",),
            vmem_limit_bytes=VMEM_LIMIT),
        name="cast",
    )(*arrays)


def kernel(x, norm1_g, w_in, attn_rel_bias, ssm_conv_w, ssm_conv_b, ssm_dt_bias, ssm_a_log, ssm_d, ssm_norm_g, conv_dw_w, conv_dw_b, conv_ln_g, conv_ln_b, w_out, norm2_g, ffn_w_gate, ffn_w_up, ffn_w_down, final_norm_g):
    bsz, s, d = x.shape
    depth = w_in.shape[0]
    t = bsz * s
    assert d == D_MODEL and s % TS == 0
    tiles_per_seq = s // TS
    row = lambda v: v.reshape(1, -1).astype(F32)
    per_head = lambda v: jnp.repeat(v.astype(F32), B_HEAD_DIM).reshape(1, B_DIM)
    rows8 = lambda w: jnp.repeat(w.astype(F32), 8, axis=0)

    flat = lambda w: w.astype(F32).reshape(-1, w.shape[-1])
    wg_b, wu_b, win_b, wout_b = _cast_call(
        [flat(ffn_w_gate), flat(ffn_w_up), flat(w_in), flat(w_out)], n_steps=8, relayout=2)
    (wd_b,) = _cast_call([flat(ffn_w_down)], n_steps=8)
    per_layer = lambda w: w.reshape(depth, -1, w.shape[-1])
    wg_b, wu_b, win_b, wout_b, wd_b = map(per_layer, (wg_b, wu_b, win_b, wout_b, wd_b))

    def proj_args(l):
        dtb = jnp.pad(ssm_dt_bias[l].astype(F32), (0, DT_PAD - B_HEADS)).reshape(1, DT_PAD)
        return dict(
            proj=(row(norm1_g[l]), (win_b, l), dtb, per_head(ssm_a_log[l])),
            prep=(rows8(ssm_conv_w[l]), row(ssm_conv_b[l]),
                  per_head(ssm_d[l]), rows8(conv_dw_w[l]),
                  row(conv_dw_b[l]),
                  row(conv_ln_g[l]), row(conv_ln_b[l])))

    def ffn_args(l, mix_ab, mix_c):
        return (mix_ab.reshape(t, D_AB), mix_c, (wout_b, l), row(norm2_g[l]),
                (wg_b, l), (wu_b, l), (wd_b, l))

    x2 = x.reshape(t, d)
    acts = _dense_call(x2, tiles_per_seq, **proj_args(0))
    for l in range(depth):
        *seq_acts, mix_c = acts
        q, kv, zs, bc, xdt, xdte, scores, fstart, xsd = (a.reshape(bsz, s, -1) for a in seq_acts)
        mix_ab = _mixer_call(q, kv, zs, bc, xdt, xdte, scores, fstart, xsd,
                             attn_rel_bias[l].astype(F32), row(ssm_norm_g[l]))
        if l + 1 < depth:
            x2, *acts = _dense_call(x2, tiles_per_seq, ffn=ffn_args(l, mix_ab, mix_c),
                                    **proj_args(l + 1))
        else:
            (x2,) = _dense_call(x2, tiles_per_seq, ffn=ffn_args(l, mix_ab, mix_c),
                                final_g=row(final_norm_g))
    return x2.reshape(bsz, s, d)
```

```python
import functools

import jax
import jax.numpy as jnp
from jax import lax
from jax.experimental import pallas as pl
from jax.experimental.pallas import tpu as pltpu

F32 = jnp.float32
BF16 = jnp.bfloat16

D_MODEL = 1024
CHUNK = 64
A_HEADS = 4
A_DIM = 256
LEFT_CHUNKS = 8
LEFT = LEFT_CHUNKS * CHUNK
MAX_REL = 128
N_REL = 2 * MAX_REL + 1
B_HEADS = 8
B_HEAD_DIM = 64
B_DIM = 512
B_GROUPS = 2
B_STATE = 128
B_CONV = 4
XBC_DIM = 1024
C_DIM = 256
C_KERNEL = 31
D_MIX = 1024
D_AB = A_DIM + B_DIM
D_FF = 2816
EPS = 1e-6
NEG_INF = -1e30

OFF_QKV = 0
OFF_Z = 768
OFF_XBC = 1280
OFF_GLU = 2304
OFF_DT = 2816
N_PROJ = 2944
DT_PAD = 128

TS = 512
QB = 128
BAND = LEFT + QB
XBC_HALO = 8
GLU_HALO = 32
FF_SLICES = ((0, 512), (512, 1024), (1024, 1536), (1536, 2048), (2048, 2816))
VMEM_LIMIT = 56 * 1024 * 1024


def _dot(a, b):
    return jnp.dot(a, b, preferred_element_type=F32)


def _dot_t(a, b):
    return lax.dot_general(a, b, (((1,), (1,)), ((), ())), preferred_element_type=F32)


def _tdot(a, b):
    return lax.dot_general(a, b, (((0,), (0,)), ((), ())), preferred_element_type=F32)


def _sel_left(m01, a):
    hi = a.astype(BF16)
    lo = (a - hi.astype(F32)).astype(BF16)
    return _dot(m01, lo) + _dot(m01, hi)


def _rms(x, g):
    return x * lax.rsqrt(jnp.mean(x * x, axis=-1, keepdims=True) + EPS) * g


def _silu(x):
    return x * jax.nn.sigmoid(x)


def _mul_rows8(x, w8):
    n = x.shape[0] // 8
    return (x.reshape(n, 8, x.shape[1]) * w8[None]).reshape(x.shape)


def _iota(shape, dim):
    return lax.broadcasted_iota(jnp.int32, shape, dim)


def _mixer_prep(first, xbc_s, glu_s, dte_s, acs_s, seg_s, convw_ref, convb_ref, dskip_ref,
                dww_ref, dwb_ref, lng_ref, lnb_ref,
                bc_ref, xdt_ref, xdte_ref, scores_ref, fstart_ref, xsd_ref, outc_ref):
    causal = _iota((CHUNK, B_DIM), 0) >= jnp.bitwise_and(_iota((CHUNK, B_DIM), 1), CHUNK - 1)
    all_ones = jnp.ones((C_DIM, C_DIM), BF16)

    def begin():
        xbc_s[0:XBC_HALO, :] = jnp.where(first, 0.0, xbc_s[0:XBC_HALO, :])
        glu_s[0:GLU_HALO, :] = jnp.where(first, 0.0, glu_s[0:GLU_HALO, :])

    def chunk(c):
        r0 = c * CHUNK
        rows = slice(r0, r0 + CHUNK)
        win = xbc_s[r0:r0 + XBC_HALO + CHUNK, :]
        acc = convb_ref[...] + _mul_rows8(win[XBC_HALO:, :], convw_ref[8 * (B_CONV - 1):8 * B_CONV, :])
        for k in range(B_CONV - 1):
            off = XBC_HALO - (B_CONV - 1) + k
            shifted = pltpu.roll(win, XBC_HALO + CHUNK - off, axis=0)[:CHUNK, :]
            acc = acc + _mul_rows8(shifted, convw_ref[8 * k:8 * k + 8, :])
        xc = _silu(acc)
        xs = xc[:, :B_DIM]
        bc = xc[:, B_DIM:].astype(BF16)
        bc_ref[rows, :] = bc
        decay = jnp.exp(jnp.where(causal, seg_s[rows, :], -jnp.inf))
        for g in range(B_GROUPS):
            bg = bc[:, g * B_STATE:(g + 1) * B_STATE]
            cg = bc[:, (B_GROUPS + g) * B_STATE:(B_GROUPS + g + 1) * B_STATE]
            cb = _dot_t(cg, jnp.concatenate([bg] * 4, axis=0))
            scores_ref[rows, g * 256:(g + 1) * 256] = (cb * decay[:, g * 256:(g + 1) * 256]).astype(BF16)
        acs = acs_s[rows, :]
        fstart_ref[rows, :] = jnp.exp(acs)
        xdt = xs * dte_s[rows, :]
        xdt_ref[rows, :] = xdt.astype(BF16)
        xdte_ref[rows, :] = (xdt * jnp.exp(acs[CHUNK - 1:CHUNK, :] - acs)).astype(BF16)
        xsd_ref[rows, :] = (xs * dskip_ref[...]).astype(BF16)
        acc = jnp.broadcast_to(dwb_ref[...], (CHUNK, C_DIM))
        for rho in range(8):
            nrow = CHUNK if rho == 0 else CHUNK + 8
            part = None
            for k in range(C_KERNEL):
                off = GLU_HALO - (C_KERNEL - 1) + k
                if off % 8 != rho:
                    continue
                start = r0 + off - rho
                term = _mul_rows8(glu_s[start:start + nrow, :], dww_ref[8 * k:8 * k + 8, :])
                part = term if part is None else part + term
            acc = acc + (part if rho == 0 else pltpu.roll(part, nrow - rho, axis=0)[:CHUNK, :])
        hi = acc.astype(BF16)
        mid = (acc - hi.astype(F32)).astype(BF16)
        mu = (_dot(mid, all_ones) + _dot(hi, all_ones)) * (1.0 / C_DIM)
        d = acc - mu
        var = _dot((d * d).astype(BF16), all_ones) * (1.0 / C_DIM)
        y = d * lax.rsqrt(var + EPS) * lng_ref[...] + lnb_ref[...]
        outc_ref[rows, :] = _silu(y).astype(BF16)

    def end():
        xbc_s[0:XBC_HALO, :] = xbc_s[TS:TS + XBC_HALO, :]
        glu_s[0:GLU_HALO, :] = glu_s[TS:TS + GLU_HALO, :]

    return begin, chunk, end


def _dense_kernel(*refs, has_ffn, has_proj, final, tiles_per_seq):
    it = iter(refs)
    take = lambda n: [next(it) for _ in range(n)]
    (x_ref,) = take(1)
    if has_ffn:
        mixab_ref, mixc_ref, wout_ref, n2g_ref, wg_ref, wu_ref, wd_ref = take(7)
    if final:
        (fg_ref,) = take(1)
    if has_proj:
        n1g_ref, win_ref, dtb_ref, alog_ref = take(4)
        prep_params = take(7)
    if has_ffn:
        (xo_ref,) = take(1)
    if has_proj:
        q_ref, kv_ref, zs_ref = take(3)
        prep_outs = take(7)
        scratch = take(5)
        xbc_s, glu_s, dte_s, acs_s, seg_s = scratch

    if has_proj:
        i = pl.program_id(0)
        n_tiles = pl.num_programs(0) - 1

        @pl.when(i == 0)
        def _init():
            for ref in scratch:
                ref[...] = jnp.zeros(ref.shape, F32)

        first = lax.rem(i + (tiles_per_seq - 1), tiles_per_seq) == 0

    def step(project):
        pending = []
        if has_proj:
            prep_begin, prep_chunk, prep_end = _mixer_prep(first, *scratch, *prep_params, *prep_outs)
            prep_begin()
            pending = list(range(TS // CHUNK))

        def prep_some(n):
            for _ in range(min(n, len(pending))):
                prep_chunk(pending.pop(0))

        if not project:
            prep_some(len(pending))
            return
        x = x_ref[...]
        if has_ffn:
            mix = jnp.concatenate([mixab_ref[...], mixc_ref[...]], axis=1)
            xn = x + _dot(mix, wout_ref[...])
            h = _rms(xn, n2g_ref[...]).astype(BF16)
            x = xn
            for lo, hi in FF_SLICES:
                g = _dot(h, wg_ref[:, lo:hi])
                u = _dot(h, wu_ref[:, lo:hi])
                x = x + _dot((_silu(g) * u).astype(BF16), wd_ref[lo:hi, :])
                prep_some(2)
            xo_ref[...] = _rms(x, fg_ref[...]) if final else x
        if has_proj:
            h = _rms(x, n1g_ref[...]).astype(BF16)
            dt_raw = _dot(h, win_ref[:, OFF_DT:N_PROJ]) + dtb_ref[...]
            dt = jnp.maximum(dt_raw, 0.0) + jnp.log1p(jnp.exp(-jnp.abs(dt_raw)))
            prep_some(len(pending))
            prep_end()
            qkv = _dot(h, win_ref[:, OFF_QKV:OFF_Z])
            q_ref[...] = (qkv[:, :A_DIM] * 0.125).astype(BF16)
            kv_ref[...] = qkv[:, A_DIM:].astype(BF16)
            low_half = _iota((TS, 128), 1) < B_HEAD_DIM
            for pr in range(B_HEADS // 2):
                dte_s[:, pr * 128:(pr + 1) * 128] = jnp.where(
                    low_half,
                    jnp.broadcast_to(dt[:, 2 * pr:2 * pr + 1], (TS, 128)),
                    jnp.broadcast_to(dt[:, 2 * pr + 1:2 * pr + 2], (TS, 128)))
            zs_ref[...] = _silu(_dot(h, win_ref[:, OFF_Z:OFF_XBC])).astype(BF16)
            a_neg = -jnp.exp(alog_ref[...])
            l_i = _iota((CHUNK, B_DIM), 0)
            s_i = jnp.bitwise_and(_iota((CHUNK, B_DIM), 1), CHUNK - 1)
            diag = (l_i == s_i).astype(F32)
            tri = (_iota((CHUNK, CHUNK), 0) >= _iota((CHUNK, CHUNK), 1)).astype(BF16)
            ones = jnp.ones((CHUNK, CHUNK), BF16)
            chunk_rows = [slice(c * CHUNK, (c + 1) * CHUNK) for c in range(TS // CHUNK)]
            for rows in chunk_rows:
                acs_s[rows, :] = _sel_left(tri, dte_s[rows, :] * a_neg)
            xbc_s[XBC_HALO:XBC_HALO + TS, :] = _dot(h, win_ref[:, OFF_XBC:OFF_GLU])
            for rows in chunk_rows:
                acs_c = acs_s[rows, :]
                seg_s[rows, :] = acs_c - _sel_left(ones, acs_c * diag)
            u = _dot(h, win_ref[:, OFF_GLU:OFF_DT])
            glu_s[GLU_HALO:GLU_HALO + TS, :] = u[:, :C_DIM] * jax.nn.sigmoid(u[:, C_DIM:])

    if has_proj:
        pl.when(i < n_tiles)(lambda: step(True))
        pl.when(i == n_tiles)(lambda: step(False))
    else:
        step(True)


def _dense_call(x2, tiles_per_seq, ffn=None, final_g=None, proj=None, prep=None):
    t = x2.shape[0]
    n_tiles = t // TS
    has_ffn, has_proj, final = ffn is not None, proj is not None, final_g is not None
    last = n_tiles - 1

    def now(c):
        return pl.BlockSpec((TS, c), lambda i: (jnp.minimum(i, last), 0))

    def before(c):
        return pl.BlockSpec((TS, c), lambda i: (jnp.maximum(i - 1, 0), 0))

    def resident(a):
        if isinstance(a, tuple):
            w, layer = a
            return pl.BlockSpec((None,) + w.shape[1:], lambda i: (layer, 0, 0),
                                pipeline_mode=pl.Buffered(1))
        return pl.BlockSpec(a.shape, lambda i: (0, 0), pipeline_mode=pl.Buffered(1))

    unwrap = lambda a: a[0] if isinstance(a, tuple) else a
    args, in_specs, out_shape, out_specs, scratch = [x2], [now(D_MODEL)], [], [], []
    if has_ffn:
        args += [unwrap(a) for a in ffn]
        in_specs += [now(D_AB), now(C_DIM)] + [resident(a) for a in ffn[2:]]
        out_shape.append(jax.ShapeDtypeStruct((t, D_MODEL), F32))
        out_specs.append(now(D_MODEL))
    if final:
        args.append(final_g)
        in_specs.append(resident(final_g))
    if has_proj:
        args += [unwrap(a) for a in list(proj) + list(prep)]
        in_specs += [resident(a) for a in list(proj) + list(prep)]
        for c, dt in ((A_DIM, BF16), (2 * A_DIM, BF16), (B_DIM, BF16)):
            out_shape.append(jax.ShapeDtypeStruct((t, c), dt))
            out_specs.append(now(c))
        for c, dt in ((B_DIM, BF16), (B_DIM, BF16), (B_DIM, BF16), (B_DIM, BF16), (B_DIM, F32),
                      (B_DIM, BF16), (C_DIM, BF16)):
            out_shape.append(jax.ShapeDtypeStruct((t, c), dt))
            out_specs.append(before(c))
        scratch = [pltpu.VMEM((XBC_HALO + TS, XBC_DIM), F32),
                   pltpu.VMEM((GLU_HALO + TS, C_DIM), F32),
                   pltpu.VMEM((TS, B_DIM), F32),
                   pltpu.VMEM((TS, B_DIM), F32),
                   pltpu.VMEM((TS, B_DIM), F32)]
    return pl.pallas_call(
        functools.partial(_dense_kernel, has_ffn=has_ffn, has_proj=has_proj, final=final,
                          tiles_per_seq=tiles_per_seq),
        grid=(n_tiles + (1 if has_proj else 0),),
        in_specs=in_specs,
        out_specs=out_specs,
        out_shape=out_shape,
        scratch_shapes=scratch,
        compiler_params=pltpu.CompilerParams(
            dimension_semantics=("arbitrary",),
            vmem_limit_bytes=VMEM_LIMIT,
            ),
        name="dense_" + ("f" if has_ffn else "") + ("p" if has_proj else "") + ("n" if final else ""),
    )(*args)


def _mixer_kernel(q_ref, kv_ref, kvp_ref, zs_ref, bc_ref, xdt_ref, xdte_ref, scores_ref, fstart_ref,
                  xsd_ref, relb_ref, normg_ref,
                  o_ref,
                  bias_s, st_s):
    b = pl.program_id(0)
    i = pl.program_id(1)

    @pl.when(jnp.logical_and(b == 0, i == 0))
    def _build_bias():
        gvw = BAND + QB
        d = _iota((8, gvw), 1) - _iota((8, gvw), 0) - (QB - 1)
        idx = jnp.clip(LEFT - d, -MAX_REL, MAX_REL) + MAX_REL
        rr = _iota((2 * QB, BAND), 0)
        c = _iota((2 * QB, BAND), 1)
        qc = jnp.right_shift(jnp.bitwise_and(rr, QB - 1), 6)
        kc = jnp.right_shift(c, 6)
        valid = jnp.logical_and(kc >= qc, kc <= qc + LEFT_CHUNKS)
        for pr in range(2):
            halves = []
            for head in (2 * pr, 2 * pr + 1):
                gv8 = lax.fori_loop(
                    0, N_REL, lambda v, acc, head=head: jnp.where(idx == v, relb_ref[head, v], acc),
                    jnp.zeros((8, gvw), F32))
                halves += [gv8[:, QB - 1 - 8 * blk:QB - 1 - 8 * blk + BAND] for blk in range(QB // 8)]
            bias_s[pr] = jnp.where(valid, jnp.concatenate(halves, axis=0), NEG_INF)

    @pl.when(i == 0)
    def _reset():
        st_s[...] = jnp.zeros((B_GROUPS, B_STATE, 256), F32)

    lane = _iota((QB, 128), 1)
    first_head = lane < 64
    def attn_scores(j, pr):
        r0 = j * QB
        qa = q_ref[0, r0:r0 + QB, pr * 128:(pr + 1) * 128]
        qq = jnp.concatenate([jnp.where(first_head, qa, jnp.zeros_like(qa)),
                              jnp.where(first_head, jnp.zeros_like(qa), qa)], axis=0)
        kcol = slice(pr * 128, (pr + 1) * 128)
        kb = jnp.concatenate([kvp_ref[0, r0:TS, kcol], kv_ref[0, 0:r0 + QB, kcol]], axis=0)
        return _dot_t(qq, kb)

    def attn_finish(j, pr, s):
        r0 = j * QB
        col_ok = (_iota((1, BAND), 1) + (i * TS + r0 - LEFT)) >= 0
        s = jnp.where(col_ok, s + bias_s[pr], NEG_INF)
        m = jnp.max(s, axis=-1, keepdims=True)
        p = jnp.exp(s - m)
        l = jnp.sum(p, axis=-1, keepdims=True)
        vcol = slice(A_DIM + pr * 128, A_DIM + (pr + 1) * 128)
        vb = jnp.concatenate([kvp_ref[0, r0:TS, vcol], kv_ref[0, 0:r0 + QB, vcol]], axis=0)
        o = _dot(p.astype(BF16), vb) / l
        o_ref[0, r0:r0 + QB, pr * 128:(pr + 1) * 128] = jnp.where(
            first_head, o[:QB], o[QB:]).astype(BF16)

    head_of_lane = jnp.right_shift(_iota((1, 256), 1), 6)
    head_mask = [(head_of_lane == hh).astype(BF16) for hh in range(4)]

    def ssd_stateless(c):
        rows = slice(c * CHUNK, (c + 1) * CHUNK)
        out = []
        for g in range(B_GROUPS):
            cols = slice(g * 256, (g + 1) * 256)
            bg = bc_ref[0, rows, g * B_STATE:(g + 1) * B_STATE]
            xg = xdt_ref[0, rows, cols]
            x_bd = jnp.concatenate([xg * head_mask[hh] for hh in range(4)], axis=0)
            out.append((_dot(scores_ref[0, rows, cols], x_bd), _tdot(bg, xdte_ref[0, rows, cols])))
        return out

    def ssd_scan(c, stateless):
        r0 = c * CHUNK
        rows = slice(r0, r0 + CHUNK)
        for g in range(B_GROUPS):
            cols = slice(g * 256, (g + 1) * 256)
            y_intra, state_in = stateless[g]
            cg = bc_ref[0, rows, (B_GROUPS + g) * B_STATE:(B_GROUPS + g + 1) * B_STATE]
            state = st_s[g]
            y = y_intra + _dot(cg, state.astype(BF16)) * fstart_ref[0, rows, cols]
            chunk_decay = fstart_ref[0, r0 + CHUNK - 1:r0 + CHUNK, cols]
            st_s[g] = state * chunk_decay + state_in
            y = (y + xsd_ref[0, rows, cols]) * zs_ref[0, rows, cols]
            y = y * lax.rsqrt(jnp.mean(y * y, axis=-1, keepdims=True) + EPS) * normg_ref[:, cols]
            o_ref[0, rows, A_DIM + g * 256:A_DIM + (g + 1) * 256] = y.astype(BF16)

    units = [(j, pr) for j in range(TS // QB) for pr in range(2)]
    n_chunks = TS // CHUNK
    assert len(units) == n_chunks
    s_next = attn_scores(*units[0])
    stateless_next = ssd_stateless(0)
    for k in range(n_chunks):
        s_cur, stateless_cur = s_next, stateless_next
        if k + 1 < n_chunks:
            s_next = attn_scores(*units[k + 1])
        attn_finish(*units[k], s_cur)
        if k + 1 < n_chunks:
            stateless_next = ssd_stateless(k + 1)
        ssd_scan(k, stateless_cur)


def _mixer_call(q, kv, zs, bc, xdt, xdte, scores, fstart, xsd, relb, normg):
    bsz, s, _ = q.shape

    def tile(c):
        return pl.BlockSpec((1, TS, c), lambda b, i: (b, i, 0))

    in_specs = [
        tile(A_DIM), tile(2 * A_DIM),
        pl.BlockSpec((1, TS, 2 * A_DIM), lambda b, i: (b, jnp.maximum(i - 1, 0), 0)),
        tile(B_DIM), tile(B_DIM), tile(B_DIM), tile(B_DIM), tile(B_DIM), tile(B_DIM), tile(B_DIM),
        pl.BlockSpec(memory_space=pltpu.SMEM),
        pl.BlockSpec(normg.shape, lambda b, i: (0, 0)),
    ]
    scratch = [
        pltpu.VMEM((2, 2 * QB, BAND), F32),
        pltpu.VMEM((B_GROUPS, B_STATE, 256), F32),
    ]
    return pl.pallas_call(
        _mixer_kernel,
        grid=(bsz, s // TS),
        in_specs=in_specs,
        out_specs=pl.BlockSpec((1, TS, D_AB), lambda b, i: (b, i, 0)),
        out_shape=jax.ShapeDtypeStruct((bsz, s, D_AB), BF16),
        scratch_shapes=scratch,
        compiler_params=pltpu.CompilerParams(
            dimension_semantics=("arbitrary", "arbitrary"),
            vmem_limit_bytes=VMEM_LIMIT),
        name="mixer",
    )(q, kv, kv, zs, bc, xdt, xdte, scores, fstart, xsd, relb, normg)


def _cast_kernel(*refs, n, relayout):
    for k, (i_ref, o_ref) in enumerate(zip(refs[:n], refs[n:])):
        if k == relayout:
            w = i_ref[...]
            o_ref[:, 0:OFF_GLU] = w[:, 0:OFF_GLU].astype(BF16)
            o_ref[:, OFF_GLU:OFF_DT] = w[:, OFF_GLU + B_HEADS:].astype(BF16)
            o_ref[:, OFF_DT:N_PROJ] = jnp.concatenate(
                [w[:, OFF_GLU:OFF_GLU + B_HEADS], jnp.zeros((w.shape[0], DT_PAD - B_HEADS), F32)],
                axis=1).astype(BF16)
        else:
            o_ref[...] = i_ref[...].astype(BF16)


def _cast_call(arrays, n_steps, relayout=None):
    depth, rows, _ = arrays[0].shape
    br = rows // n_steps
    out_cols = [N_PROJ if k == relayout else a.shape[2] for k, a in enumerate(arrays)]
    return pl.pallas_call(
        functools.partial(_cast_kernel, n=len(arrays), relayout=relayout),
        grid=(depth, n_steps),
        in_specs=[pl.BlockSpec((None, br, a.shape[2]), lambda l, i: (l, i, 0)) for a in arrays],
        out_specs=[pl.BlockSpec((None, br, c), lambda l, i: (l, i, 0)) for c in out_cols],
        out_shape=[jax.ShapeDtypeStruct((depth, rows, c), BF16) for c in out_cols],
        compiler_params=pltpu.CompilerParams(
            dimension_semantics=("arbitrary", "arbitrary"),
            vmem_limit_bytes=VMEM_LIMIT),
        name="cast",
    )(*arrays)


def kernel(x, norm1_g, w_in, attn_rel_bias, ssm_conv_w, ssm_conv_b, ssm_dt_bias, ssm_a_log, ssm_d, ssm_norm_g, conv_dw_w, conv_dw_b, conv_ln_g, conv_ln_b, w_out, norm2_g, ffn_w_gate, ffn_w_up, ffn_w_down, final_norm_g):
    bsz, s, d = x.shape
    depth = w_in.shape[0]
    t = bsz * s
    assert d == D_MODEL and s % TS == 0
    tiles_per_seq = s // TS
    row = lambda v: v.reshape(1, -1).astype(F32)
    per_head = lambda v: jnp.repeat(v.astype(F32), B_HEAD_DIM).reshape(1, B_DIM)
    rows8 = lambda w: jnp.repeat(w.astype(F32), 8, axis=0)

    f32 = lambda w: w.astype(F32)
    wg_b, wu_b, win_b, wout_b = _cast_call(
        [f32(ffn_w_gate), f32(ffn_w_up), f32(w_in), f32(w_out)], n_steps=4, relayout=2)
    (wd_b,) = _cast_call([f32(ffn_w_down)], n_steps=4)

    def proj_args(l):
        dtb = jnp.pad(ssm_dt_bias[l].astype(F32), (0, DT_PAD - B_HEADS)).reshape(1, DT_PAD)
        return dict(
            proj=(row(norm1_g[l]), (win_b, l), dtb, per_head(ssm_a_log[l])),
            prep=(rows8(ssm_conv_w[l]), row(ssm_conv_b[l]),
                  per_head(ssm_d[l]), rows8(conv_dw_w[l]),
                  row(conv_dw_b[l]),
                  row(conv_ln_g[l]), row(conv_ln_b[l])))

    def ffn_args(l, mix_ab, mix_c):
        return (mix_ab.reshape(t, D_AB), mix_c, (wout_b, l), row(norm2_g[l]),
                (wg_b, l), (wu_b, l), (wd_b, l))

    x2 = x.reshape(t, d)
    acts = _dense_call(x2, tiles_per_seq, **proj_args(0))
    for l in range(depth):
        *seq_acts, mix_c = acts
        q, kv, zs, bc, xdt, xdte, scores, fstart, xsd = (a.reshape(bsz, s, -1) for a in seq_acts)
        mix_ab = _mixer_call(q, kv, zs, bc, xdt, xdte, scores, fstart, xsd,
                             attn_rel_bias[l].astype(F32), row(ssm_norm_g[l]))
        if l + 1 < depth:
            x2, *acts = _dense_call(x2, tiles_per_seq, ffn=ffn_args(l, mix_ab, mix_c),
                                    **proj_args(l + 1))
        else:
            (x2,) = _dense_call(x2, tiles_per_seq, ffn=ffn_args(l, mix_ab, mix_c),
                                final_g=row(final_norm_g))
    return x2.reshape(bsz, s, d)
```
